```python
import jax, jax.numpy as jnp
from jax import lax
import numpy as np

D_MODEL = 1024
BATCH = 8
SEQ = 8192
DEPTH = 4
DEC_BATCH = 16
DEC_SEQ = 2048
PAST_LEN = 128

N_MIXERS = 2
N_A_LAYERS = (DEPTH + 1) // 2
N_B_LAYERS = DEPTH // 2
PLE_DIM = 256
RMS_EPS = 1e-6
ROPE_THETA = 500000.0
BLOCK = 128
NEG_INF = -1e30
A_HEADS = 16
A_KV_HEADS = 4
A_GROUP = A_HEADS // A_KV_HEADS
A_HEAD_DIM = 64
A_ROT_DIM = A_HEAD_DIM // 4
WINDOW = 128
A_QKV_DIM = (A_HEADS + 2 * A_KV_HEADS) * A_HEAD_DIM
B_HEADS = 16
B_Q_RANK = 384
B_KV_RANK = 128
B_NOPE_DIM = 64
B_ROPE_DIM = 32
B_V_DIM = 64
B_QK_DIM = B_NOPE_DIM + B_ROPE_DIM
B_IN_DIM = B_Q_RANK + B_KV_RANK + B_ROPE_DIM
FFN_HIDDEN = ((8 * D_MODEL + 3 * 256 - 1) // (3 * 256)) * 256

kernel_name = "hybrid_swa_sink_mla_encoder"


def rmsnorm(x, g):
    xf = x.astype(jnp.float32)
    y = xf * lax.rsqrt(jnp.mean(xf * xf, axis=-1, keepdims=True) + RMS_EPS)
    return (y * g.astype(jnp.float32)).astype(x.dtype)


def rope_tables(seq, rot_dim, dtype):
    inv = 1.0 / (ROPE_THETA ** (jnp.arange(0, rot_dim, 2, dtype=jnp.float32) / rot_dim))
    ang = jnp.arange(seq, dtype=jnp.float32)[:, None] * inv[None, :]
    return jnp.cos(ang).astype(dtype), jnp.sin(ang).astype(dtype)


def apply_rope(x, cos, sin):
    half = x.shape[-1] // 2
    x1, x2 = x[..., :half], x[..., half:]
    c = cos[:, None, :]
    s = sin[:, None, :]
    return jnp.concatenate([x1 * c - x2 * s, x1 * s + x2 * c], axis=-1)


def windowed_gqa(h, w_qkv, q_gain, k_gain, sink, w_o):
    B, S, _ = h.shape
    nb = S // BLOCK
    qkv = h @ w_qkv
    q = qkv[..., :A_HEADS * A_HEAD_DIM].reshape(B, S, A_HEADS, A_HEAD_DIM)
    k = qkv[..., A_HEADS * A_HEAD_DIM:(A_HEADS + A_KV_HEADS) * A_HEAD_DIM].reshape(B, S, A_KV_HEADS, A_HEAD_DIM)
    v = qkv[..., (A_HEADS + A_KV_HEADS) * A_HEAD_DIM:].reshape(B, S, A_KV_HEADS, A_HEAD_DIM)
    q = rmsnorm(q, q_gain)
    k = rmsnorm(k, k_gain)
    cos, sin = rope_tables(S, A_ROT_DIM, h.dtype)
    q = jnp.concatenate([apply_rope(q[..., :A_ROT_DIM], cos, sin), q[..., A_ROT_DIM:]], axis=-1)
    k = jnp.concatenate([apply_rope(k[..., :A_ROT_DIM], cos, sin), k[..., A_ROT_DIM:]], axis=-1)

    qb = q.reshape(B, nb, BLOCK, A_KV_HEADS, A_GROUP, A_HEAD_DIM)
    pad = ((0, 0), (BLOCK, BLOCK), (0, 0), (0, 0))
    kp = jnp.pad(k, pad).reshape(B, nb + 2, BLOCK, A_KV_HEADS, A_HEAD_DIM)
    vp = jnp.pad(v, pad).reshape(B, nb + 2, BLOCK, A_KV_HEADS, A_HEAD_DIM)
    kn = jnp.concatenate([kp[:, :-2], kp[:, 1:-1], kp[:, 2:]], axis=2)
    vn = jnp.concatenate([vp[:, :-2], vp[:, 1:-1], vp[:, 2:]], axis=2)

    scale = A_HEAD_DIM ** -0.5
    scores = jnp.einsum('bnqkgd,bnjkd->bnkgqj', qb, kn).astype(jnp.float32) * scale
    qi = jnp.arange(BLOCK, dtype=jnp.int32)[:, None]
    kj = jnp.arange(3 * BLOCK, dtype=jnp.int32)[None, :]
    in_window = jnp.abs(kj - BLOCK - qi) <= WINDOW
    key_pos = (jnp.arange(nb, dtype=jnp.int32)[:, None] - 1) * BLOCK + kj
    in_range = (key_pos >= 0) & (key_pos < S)
    mask = in_window[None, :, :] & in_range[:, None, :]
    scores = jnp.where(mask[None, :, None, None, :, :], scores, NEG_INF)

    sink_b = sink.astype(jnp.float32).reshape(A_KV_HEADS, A_GROUP)[None, None, :, :, None, None]
    m = jnp.maximum(jnp.max(scores, axis=-1, keepdims=True), sink_b)
    e = jnp.exp(scores - m)
    denom = jnp.sum(e, axis=-1, keepdims=True) + jnp.exp(sink_b - m)
    probs = (e / denom).astype(v.dtype)
    out = jnp.einsum('bnkgqj,bnjkd->bnqkgd', probs, vn)
    return out.reshape(B, S, A_HEADS * A_HEAD_DIM) @ w_o


def mla(h, w_in, q_lat_gain, kv_lat_gain, w_uq, w_ukv, q_gain, k_gain, w_o):
    B, S, _ = h.shape
    nb = S // BLOCK
    lat = h @ w_in
    cq = rmsnorm(lat[..., :B_Q_RANK], q_lat_gain)
    ckv = rmsnorm(lat[..., B_Q_RANK:B_Q_RANK + B_KV_RANK], kv_lat_gain)
    k_rope = lat[..., B_Q_RANK + B_KV_RANK:]
    q = (cq @ w_uq).reshape(B, S, B_HEADS, B_QK_DIM)
    kv = (ckv @ w_ukv).reshape(B, S, B_HEADS, B_NOPE_DIM + B_V_DIM)
    k_nope, v = kv[..., :B_NOPE_DIM], kv[..., B_NOPE_DIM:]
    k_r = jnp.broadcast_to(k_rope[:, :, None, :], (B, S, B_HEADS, B_ROPE_DIM))
    k = jnp.concatenate([k_nope, k_r], axis=-1)
    q = rmsnorm(q, q_gain)
    k = rmsnorm(k, k_gain)
    cos, sin = rope_tables(S, B_ROPE_DIM, h.dtype)
    q = jnp.concatenate([q[..., :B_NOPE_DIM], apply_rope(q[..., B_NOPE_DIM:], cos, sin)], axis=-1)
    k = jnp.concatenate([k[..., :B_NOPE_DIM], apply_rope(k[..., B_NOPE_DIM:], cos, sin)], axis=-1)

    scale = B_QK_DIM ** -0.5
    qb = q.reshape(B, nb, BLOCK, B_HEADS, B_QK_DIM).transpose(1, 0, 2, 3, 4)

    def attend(qblk):
        s = jnp.einsum('bqhd,bkhd->bhqk', qblk, k).astype(jnp.float32) * scale
        p = jax.nn.softmax(s, axis=-1).astype(v.dtype)
        return jnp.einsum('bhqk,bkhd->bqhd', p, v)

    o = lax.map(attend, qb)
    o = o.transpose(1, 0, 2, 3, 4).reshape(B, S, B_HEADS * B_V_DIM)
    return o @ w_o


def swiglu(h, w_gate_up, w_down):
    gu = h @ w_gate_up
    return (jax.nn.silu(gu[..., :FFN_HIDDEN]) * gu[..., FFN_HIDDEN:]) @ w_down


def trunk(x, p, mix_norm, a_w_qkv, a_q_norm, a_k_norm, a_sink, a_w_o,
          b_w_in, b_q_lat_norm, b_kv_lat_norm, b_w_uq, b_w_ukv, b_q_norm, b_k_norm, b_w_o,
          ffn_norm, ffn_w_gate_up, ffn_w_down, ple_norm, ple_w_gate, ple_w_proj):
    for i in range(DEPTH):
        j = i // N_MIXERS
        h = rmsnorm(x, mix_norm[i])
        if i % N_MIXERS == 0:
            x = x + windowed_gqa(h, a_w_qkv[j], a_q_norm[j], a_k_norm[j], a_sink[j], a_w_o[j])
        else:
            x = x + mla(h, b_w_in[j], b_q_lat_norm[j], b_kv_lat_norm[j], b_w_uq[j], b_w_ukv[j],
                        b_q_norm[j], b_k_norm[j], b_w_o[j])
        x = x + swiglu(rmsnorm(x, ffn_norm[i]), ffn_w_gate_up[i], ffn_w_down[i])
        gate = jax.nn.sigmoid(rmsnorm(x, ple_norm[i]) @ ple_w_gate[i])
        x = x + gate * (p[i] @ ple_w_proj[i])
    return x


def setup_inputs(seed: int = 0) -> dict:
    key = jax.random.key(seed)
    ks = jax.random.split(key, 32)
    f32 = jnp.float32

    def w(k, shape, fan_in):
        return jax.random.normal(k, shape, f32) * (fan_in ** -0.5)

    def gain(k, shape):
        return 1.0 + 0.02 * jax.random.normal(k, shape, f32)

    return {
        "x_prompt": jax.random.normal(ks[0], (BATCH, SEQ, D_MODEL), f32),
        "x_sample": jax.random.normal(ks[1], (DEC_BATCH, DEC_SEQ, D_MODEL), f32),
        "p_prompt": jax.random.normal(ks[2], (DEPTH, BATCH, SEQ, PLE_DIM), f32),
        "p_sample": jax.random.normal(ks[3], (DEPTH, DEC_BATCH, DEC_SEQ, PLE_DIM), f32),
        "mix_norm": gain(ks[4], (DEPTH, D_MODEL)),
        "a_w_qkv": w(ks[5], (N_A_LAYERS, D_MODEL, A_QKV_DIM), D_MODEL),
        "a_q_norm": gain(ks[6], (N_A_LAYERS, A_HEAD_DIM)),
        "a_k_norm": gain(ks[7], (N_A_LAYERS, A_HEAD_DIM)),
        "a_sink": 0.5 * jax.random.normal(ks[8], (N_A_LAYERS, A_HEADS), f32),
        "a_w_o": w(ks[9], (N_A_LAYERS, A_HEADS * A_HEAD_DIM, D_MODEL), A_HEADS * A_HEAD_DIM),
        "b_w_in": w(ks[10], (N_B_LAYERS, D_MODEL, B_IN_DIM), D_MODEL),
        "b_q_lat_norm": gain(ks[11], (N_B_LAYERS, B_Q_RANK)),
        "b_kv_lat_norm": gain(ks[12], (N_B_LAYERS, B_KV_RANK)),
        "b_w_uq": w(ks[13], (N_B_LAYERS, B_Q_RANK, B_HEADS * B_QK_DIM), B_Q_RANK),
        "b_w_ukv": w(ks[14], (N_B_LAYERS, B_KV_RANK, B_HEADS * (B_NOPE_DIM + B_V_DIM)), B_KV_RANK),
        "b_q_norm": gain(ks[15], (N_B_LAYERS, B_QK_DIM)),
        "b_k_norm": gain(ks[16], (N_B_LAYERS, B_QK_DIM)),
        "b_w_o": w(ks[17], (N_B_LAYERS, B_HEADS * B_V_DIM, D_MODEL), B_HEADS * B_V_DIM),
        "ffn_norm": gain(ks[18], (DEPTH, D_MODEL)),
        "ffn_w_gate_up": w(ks[19], (DEPTH, D_MODEL, 2 * FFN_HIDDEN), D_MODEL),
        "ffn_w_down": w(ks[20], (DEPTH, FFN_HIDDEN, D_MODEL), FFN_HIDDEN),
        "ple_norm": gain(ks[21], (DEPTH, D_MODEL)),
        "ple_w_gate": w(ks[22], (DEPTH, D_MODEL, D_MODEL), D_MODEL),
        "ple_w_proj": w(ks[23], (DEPTH, PLE_DIM, D_MODEL), PLE_DIM),
    }


def reference(x_prompt, x_sample, p_prompt, p_sample, mix_norm, a_w_qkv, a_q_norm, a_k_norm,
              a_sink, a_w_o, b_w_in, b_q_lat_norm, b_kv_lat_norm, b_w_uq, b_w_ukv, b_q_norm,
              b_k_norm, b_w_o, ffn_norm, ffn_w_gate_up, ffn_w_down, ple_norm, ple_w_gate,
              ple_w_proj):
    y_prompt = trunk(x_prompt, p_prompt, mix_norm, a_w_qkv, a_q_norm, a_k_norm, a_sink, a_w_o,
                     b_w_in, b_q_lat_norm, b_kv_lat_norm, b_w_uq, b_w_ukv, b_q_norm, b_k_norm, b_w_o,
                     ffn_norm, ffn_w_gate_up, ffn_w_down, ple_norm, ple_w_gate, ple_w_proj)
    y_sample = trunk(x_sample, p_sample, mix_norm, a_w_qkv, a_q_norm, a_k_norm, a_sink, a_w_o,
                     b_w_in, b_q_lat_norm, b_kv_lat_norm, b_w_uq, b_w_ukv, b_q_norm, b_k_norm, b_w_o,
                     ffn_norm, ffn_w_gate_up, ffn_w_down, ple_norm, ple_w_gate, ple_w_proj)
    return (y_prompt, y_sample)
```

```python
import functools
import math

import jax
import jax.numpy as jnp
from jax import lax
from jax.experimental import pallas as pl
from jax.experimental.pallas import tpu as pltpu

F32 = jnp.float32
BF16 = jnp.bfloat16

D_MODEL = 1024
PLE_DIM = 256
RMS_EPS = 1e-6
ROPE_THETA = 500000.0
BLOCK = 128
A_HEADS = 16
A_KV_HEADS = 4
A_HEAD_DIM = 64
A_ROT_DIM = 16
B_HEADS = 16
B_Q_RANK = 384
B_KV_RANK = 128
B_NOPE_DIM = 64
B_ROPE_DIM = 32
B_V_DIM = 64
B_QK_DIM = B_NOPE_DIM + B_ROPE_DIM
FFN_HIDDEN = 2816

LANES = 128
HALF = LANES // 2
VMEM_LIMIT_BYTES = 56 * 1024 * 1024

LOG2E = math.log2(math.e)
MASKED = -1e30

PROJ_TILE = 512
POST_TILE = 256
MLA_Q_TILE = 512
MLA_KV_TILE = 512


def _compiler_params(semantics):
    return pltpu.CompilerParams(dimension_semantics=semantics,
                                vmem_limit_bytes=VMEM_LIMIT_BYTES)


def _resident(shape):
    nd = len(shape)
    return pl.BlockSpec(shape, lambda *_: (0,) * nd, pipeline_mode=pl.Buffered(1))


def _rms(x, g):
    ms = jnp.mean(x * x, axis=-1, keepdims=True)
    return x * lax.rsqrt(ms + RMS_EPS) * g


def _sigmoid(x):
    return 1.0 / (1.0 + jnp.exp(-x))


def _lane_iota(shape):
    return lax.broadcasted_iota(jnp.int32, shape, len(shape) - 1)


def _rope(x, c, s_from_below, s_from_above, shift):
    return (x * c + pltpu.roll(x, shift, 1) * s_from_below
            + pltpu.roll(x, LANES - shift, 1) * s_from_above)


def _rope_tables(seq, rot_dim, lane_offsets):
    half = rot_dim // 2
    inv = 1.0 / (ROPE_THETA ** (jnp.arange(0, rot_dim, 2, dtype=F32) / rot_dim))
    ang = jnp.arange(seq, dtype=F32)[:, None] * inv[None, :]
    cos, sin = jnp.cos(ang), jnp.sin(ang)
    c = jnp.ones((seq, LANES), F32)
    lo = jnp.zeros((seq, LANES), F32)
    hi = jnp.zeros((seq, LANES), F32)
    for off in lane_offsets:
        c = c.at[:, off:off + half].set(cos).at[:, off + half:off + rot_dim].set(cos)
        hi = hi.at[:, off:off + half].set(-sin)
        lo = lo.at[:, off + half:off + rot_dim].set(sin)
    return c, lo, hi


def _proj_a_kernel(x_ref, g_ref, w_ref, gq_ref, gk_ref, c_ref, sl_ref, sh_ref,
                   q_ref, k_ref, v_ref):
    x = x_ref[0]
    h = _rms(x, g_ref[...]).astype(BF16)
    qkv = jnp.dot(h, w_ref[...], preferred_element_type=F32)
    c, sl, sh = c_ref[...], sl_ref[...], sh_ref[...]
    lane = _lane_iota((x.shape[0], LANES))
    is_lo = lane < HALF
    shift = A_ROT_DIM // 2

    def head_norm(t, gain):
        sq = t * t
        ss_lo = jnp.sum(jnp.where(is_lo, sq, 0.0), axis=-1, keepdims=True)
        ss_hi = jnp.sum(jnp.where(is_lo, 0.0, sq), axis=-1, keepdims=True)
        r = jnp.where(is_lo, lax.rsqrt(ss_lo / A_HEAD_DIM + RMS_EPS),
                      lax.rsqrt(ss_hi / A_HEAD_DIM + RMS_EPS))
        return _rope(t * r * gain, c, sl, sh, shift)

    n_q = A_HEADS * A_HEAD_DIM // LANES
    n_kv = A_KV_HEADS * A_HEAD_DIM // LANES
    for j in range(n_q):
        t = qkv[:, j * LANES:(j + 1) * LANES]
        q_ref[0, j] = head_norm(t, gq_ref[...]).astype(BF16)
    k_off = n_q * LANES
    v_off = k_off + n_kv * LANES

    def split_pair(t, out_ref, j):
        sw = pltpu.roll(t, HALF, 1)
        out_ref[0, 4 * j + 0] = jnp.where(is_lo, t, 0.0).astype(BF16)
        out_ref[0, 4 * j + 1] = jnp.where(is_lo, 0.0, sw).astype(BF16)
        out_ref[0, 4 * j + 2] = jnp.where(is_lo, sw, 0.0).astype(BF16)
        out_ref[0, 4 * j + 3] = jnp.where(is_lo, 0.0, t).astype(BF16)

    for j in range(n_kv):
        t = qkv[:, k_off + j * LANES:k_off + (j + 1) * LANES]
        split_pair(head_norm(t, gk_ref[...]), k_ref, j)
        split_pair(qkv[:, v_off + j * LANES:v_off + (j + 1) * LANES], v_ref, j)


def _proj_a(x, g, w_qkv, gq, gk, tables):
    B, S, _ = x.shape
    ts = min(PROJ_TILE, S)
    n_q = A_HEADS * A_HEAD_DIM // LANES
    n_kv2 = 2 * A_KV_HEADS
    tab = pl.BlockSpec((ts, LANES), lambda b, s: (s, 0))
    out4 = lambda n: pl.BlockSpec((1, n, ts, LANES), lambda b, s: (b, 0, s, 0))
    return pl.pallas_call(
        _proj_a_kernel,
        grid=(B, S // ts),
        in_specs=[pl.BlockSpec((1, ts, D_MODEL), lambda b, s: (b, s, 0)),
                  _resident(g.shape), _resident(w_qkv.shape),
                  _resident(gq.shape), _resident(gk.shape), tab, tab, tab],
        out_specs=[out4(n_q), out4(n_kv2), out4(n_kv2)],
        out_shape=[jax.ShapeDtypeStruct((B, n_q, S, LANES), BF16),
                   jax.ShapeDtypeStruct((B, n_kv2, S, LANES), BF16),
                   jax.ShapeDtypeStruct((B, n_kv2, S, LANES), BF16)],
        compiler_params=_compiler_params(("parallel", "parallel")),
        name="proj_a",
    )(x, g, w_qkv, gq, gk, *tables)


def _attn_a_kernel(sink_ref, q_ref, kp_ref, kc_ref, kn_ref, vp_ref, vc_ref, vn_ref, o_ref):
    g = pl.program_id(1)
    n = pl.program_id(2)
    nb = pl.num_programs(2)
    row = lax.broadcasted_iota(jnp.int32, (BLOCK, BLOCK), 0)
    col = lax.broadcasted_iota(jnp.int32, (BLOCK, BLOCK), 1)
    mask_prev = (col >= row) & (n > 0)
    mask_next = (col <= row) & (n < nb - 1)
    nt = (((1,), (1,)), ((), ()))
    for pi in range(2):
        q = q_ref[0, pi]
        acc = jnp.zeros((BLOCK, LANES), F32)
        for hf in range(2):
            sink = sink_ref[4 * g + 2 * pi + hf] * LOG2E
            s_p = lax.dot_general(q, kp_ref[0, hf], nt, preferred_element_type=F32)
            s_c = lax.dot_general(q, kc_ref[0, hf], nt, preferred_element_type=F32)
            s_n = lax.dot_general(q, kn_ref[0, hf], nt, preferred_element_type=F32)
            s_p = jnp.where(mask_prev, s_p, MASKED)
            s_n = jnp.where(mask_next, s_n, MASKED)
            m = jnp.max(jnp.maximum(jnp.maximum(s_p, s_c), s_n), axis=-1, keepdims=True)
            m = jnp.maximum(m, sink)
            e_p, e_c, e_n = jnp.exp2(s_p - m), jnp.exp2(s_c - m), jnp.exp2(s_n - m)
            denom = jnp.sum(e_p + e_c + e_n, axis=-1, keepdims=True) + jnp.exp2(sink - m)
            inv = 1.0 / denom
            acc += jnp.dot((e_p * inv).astype(BF16), vp_ref[0, hf], preferred_element_type=F32)
            acc += jnp.dot((e_c * inv).astype(BF16), vc_ref[0, hf], preferred_element_type=F32)
            acc += jnp.dot((e_n * inv).astype(BF16), vn_ref[0, hf], preferred_element_type=F32)
        o_ref[0, :, pi * LANES:(pi + 1) * LANES] = acc.astype(BF16)


def _attn_a(sink, q, k, v):
    B, _, S, _ = q.shape
    nb = S // BLOCK
    qspec = pl.BlockSpec((1, 2, BLOCK, LANES), lambda b, g, n: (b, g, n, 0))
    prev = pl.BlockSpec((1, 2, BLOCK, LANES), lambda b, g, n: (b, g, jnp.maximum(n - 1, 0), 0))
    cur = pl.BlockSpec((1, 2, BLOCK, LANES), lambda b, g, n: (b, g, n, 0))
    nxt = pl.BlockSpec((1, 2, BLOCK, LANES), lambda b, g, n: (b, g, jnp.minimum(n + 1, nb - 1), 0))
    return pl.pallas_call(
        _attn_a_kernel,
        grid=(B, A_KV_HEADS, nb),
        in_specs=[pl.BlockSpec(memory_space=pltpu.SMEM), qspec, prev, cur, nxt, prev, cur, nxt],
        out_specs=pl.BlockSpec((1, BLOCK, 2 * LANES), lambda b, g, n: (b, n, g)),
        out_shape=jax.ShapeDtypeStruct((B, S, A_HEADS * A_HEAD_DIM), BF16),
        compiler_params=_compiler_params(("parallel", "parallel", "parallel")),
        name="attn_a",
    )(sink, q, k, k, k, v, v, v)


def _proj_b_kernel(x_ref, g_ref, win_ref, gql_ref, gkvl_ref, wuq_ref, wuk_ref, wuv_ref,
                   gq_ref, gk_ref, c_ref, sl_ref, sh_ref, q_ref, k_ref, v_ref):
    x = x_ref[0]
    h = _rms(x, g_ref[...]).astype(BF16)
    lat = jnp.dot(h, win_ref[...], preferred_element_type=F32)
    cq = _rms(lat[:, :B_Q_RANK], gql_ref[...]).astype(BF16)
    ckv = _rms(lat[:, B_Q_RANK:B_Q_RANK + B_KV_RANK], gkvl_ref[...]).astype(BF16)
    k_rope = lat[:, B_Q_RANK + B_KV_RANK:]
    q_all = jnp.dot(cq, wuq_ref[...], preferred_element_type=F32)
    kn_all = jnp.dot(ckv, wuk_ref[...], preferred_element_type=F32)
    v_all = jnp.dot(ckv, wuv_ref[...], preferred_element_type=F32)
    c, sl, sh = c_ref[...], sl_ref[...], sh_ref[...]
    shift = B_ROPE_DIM // 2
    ones_lane = (_lane_iota((1, LANES)) == B_V_DIM).astype(F32)

    def head_norm(t, gain):
        ms = jnp.sum(t * t, axis=-1, keepdims=True) / B_QK_DIM
        return _rope(t * lax.rsqrt(ms + RMS_EPS) * gain, c, sl, sh, shift)

    for hd in range(B_HEADS):
        sl_h = slice(hd * LANES, (hd + 1) * LANES)
        q_ref[0, hd] = head_norm(q_all[:, sl_h], gq_ref[...]).astype(BF16)
        k_ref[0, hd] = head_norm(kn_all[:, sl_h] + k_rope, gk_ref[...]).astype(BF16)
        v_ref[0, hd] = (v_all[:, sl_h] + ones_lane).astype(BF16)


def _proj_b(x, g, w_in, gql, gkvl, w_uq, w_uk, w_uv, gq, gk, tables):
    B, S, _ = x.shape
    ts = min(PROJ_TILE, S)
    tab = pl.BlockSpec((ts, LANES), lambda b, s: (s, 0))
    out = pl.BlockSpec((1, B_HEADS, ts, LANES), lambda b, s: (b, 0, s, 0))
    shp = jax.ShapeDtypeStruct((B, B_HEADS, S, LANES), BF16)
    consts = [g, w_in, gql, gkvl, w_uq, w_uk, w_uv, gq, gk]
    return pl.pallas_call(
        _proj_b_kernel,
        grid=(B, S // ts),
        in_specs=[pl.BlockSpec((1, ts, D_MODEL), lambda b, s: (b, s, 0))]
                 + [_resident(a.shape) for a in consts] + [tab, tab, tab],
        out_specs=[out, out, out],
        out_shape=[shp, shp, shp],
        compiler_params=_compiler_params(("parallel", "parallel")),
        name="proj_b",
    )(x, *consts, *tables)


def _attn_b_kernel(q_ref, k_ref, v_ref, o_ref, *, tk):
    S = k_ref.shape[2]
    tq = q_ref.shape[2]
    nt = (((1,), (1,)), ((), ()))
    outs = []
    for hh in range(2):
        q = q_ref[0, hh]

        def body(c, carry, hh=hh, q=q):
            m, acc = carry
            start = pl.multiple_of(c * tk, tk)
            k = k_ref[0, hh, pl.ds(start, tk), :]
            v = v_ref[0, hh, pl.ds(start, tk), :]
            s = lax.dot_general(q, k, nt, preferred_element_type=F32)
            m_new = jnp.maximum(m, jnp.max(s, axis=-1, keepdims=True))
            alpha = jnp.exp2(m - m_new)
            p = jnp.exp2(s - m_new).astype(BF16)
            acc = acc * alpha + jnp.dot(p, v, preferred_element_type=F32)
            return m_new, acc

        m0 = jnp.full((tq, 1), MASKED, F32)
        acc0 = jnp.zeros((tq, LANES), F32)
        _, acc = lax.fori_loop(0, S // tk, body, (m0, acc0))
        outs.append(acc / acc[:, B_V_DIM:B_V_DIM + 1])
    is_lo = _lane_iota((tq, LANES)) < HALF
    o_ref[0] = jnp.where(is_lo, outs[0], pltpu.roll(outs[1], HALF, 1)).astype(BF16)


def _attn_b(q, k, v):
    B, H, S, _ = q.shape
    tq = min(MLA_Q_TILE, S)
    tk = min(MLA_KV_TILE, S)
    kv = pl.BlockSpec((1, 2, S, LANES), lambda b, j, i: (b, j, 0, 0))
    return pl.pallas_call(
        functools.partial(_attn_b_kernel, tk=tk),
        grid=(B, H // 2, S // tq),
        in_specs=[pl.BlockSpec((1, 2, tq, LANES), lambda b, j, i: (b, j, i, 0)), kv, kv],
        out_specs=pl.BlockSpec((1, tq, LANES), lambda b, j, i: (b, i, j)),
        out_shape=jax.ShapeDtypeStruct((B, S, H * B_V_DIM), BF16),
        compiler_params=_compiler_params(("parallel", "parallel", "arbitrary")),
        name="attn_b",
    )(q, k, v)


def _post_kernel(x_ref, o_ref, p_ref, wo_ref, fg_ref, wgu_ref, wd_ref, pg_ref, wpg_ref,
                 wpp_ref, out_ref):
    x = x_ref[...] + jnp.dot(o_ref[...], wo_ref[...], preferred_element_type=F32)
    h = _rms(x, fg_ref[...]).astype(BF16)
    gu = jnp.dot(h, wgu_ref[...], preferred_element_type=F32)
    gate, up = gu[:, :FFN_HIDDEN], gu[:, FFN_HIDDEN:]
    act = (gate * _sigmoid(gate) * up).astype(BF16)
    x = x + jnp.dot(act, wd_ref[...], preferred_element_type=F32)
    h = _rms(x, pg_ref[...]).astype(BF16)
    ple_gate = _sigmoid(jnp.dot(h, wpg_ref[...], preferred_element_type=F32))
    emb = jnp.dot(p_ref[0].astype(BF16), wpp_ref[...], preferred_element_type=F32)
    out_ref[...] = x + ple_gate * emb


def _post(x, o, p, layer, wo, fg, wgu, wd, pg, wpg, wpp):
    T = x.shape[0]
    tm = min(POST_TILE, T)
    consts = [wo, fg, wgu, wd, pg, wpg, wpp]
    row = lambda width: pl.BlockSpec((tm, width), lambda t: (t, 0))
    return pl.pallas_call(
        _post_kernel,
        grid=(T // tm,),
        in_specs=[row(D_MODEL), row(o.shape[1]),
                  pl.BlockSpec((1, tm, PLE_DIM), lambda t: (layer, t, 0))]
                 + [_resident(a.shape) for a in consts],
        out_specs=row(D_MODEL),
        out_shape=jax.ShapeDtypeStruct((T, D_MODEL), F32),
        compiler_params=_compiler_params(("parallel",)),
        name="post",
    )(x, o, p, *consts)


def _head_slots(w, heads, width):
    kdim = w.shape[0]
    w = w.reshape(kdim, heads, width)
    return jnp.pad(w, ((0, 0), (0, 0), (0, LANES - width))).reshape(kdim, heads * LANES)


def _prep_a(w_qkv, q_gain, k_gain):
    scale = (A_HEAD_DIM ** -0.5) * LOG2E
    gq = jnp.tile(q_gain, 2)[None, :] * scale
    gk = jnp.tile(k_gain, 2)[None, :]
    return w_qkv.astype(BF16), gq, gk


def _prep_b(w_in, w_uq, w_ukv, q_gain, k_gain):
    scale = (B_QK_DIM ** -0.5) * LOG2E
    rope_cols = w_in[:, B_Q_RANK + B_KV_RANK:]
    rope_slot = jnp.pad(rope_cols, ((0, 0), (B_NOPE_DIM, LANES - B_QK_DIM)))
    w_in_p = jnp.concatenate([w_in[:, :B_Q_RANK + B_KV_RANK], rope_slot], axis=1).astype(BF16)
    w_uq_p = _head_slots(w_uq, B_HEADS, B_QK_DIM).astype(BF16)
    ukv = w_ukv.reshape(B_KV_RANK, B_HEADS, B_NOPE_DIM + B_V_DIM)
    w_uk_p = _head_slots(ukv[:, :, :B_NOPE_DIM].reshape(B_KV_RANK, -1), B_HEADS, B_NOPE_DIM).astype(BF16)
    w_uv_p = _head_slots(ukv[:, :, B_NOPE_DIM:].reshape(B_KV_RANK, -1), B_HEADS, B_V_DIM).astype(BF16)
    pad = (0, LANES - B_QK_DIM)
    gq = jnp.pad(q_gain, pad)[None, :] * scale
    gk = jnp.pad(k_gain, pad)[None, :]
    return w_in_p, w_uq_p, w_uk_p, w_uv_p, gq, gk


def _trunk(x, p, mix_norm, layers_a, layers_b, a_sink, a_w_o, b_w_o, ffn_norm, w_gu, w_d,
           ple_norm, w_pg, w_pp):
    B, S, _ = x.shape
    depth = p.shape[0]
    T = B * S
    p = p.reshape(depth, T, PLE_DIM)
    tab_a = _rope_tables(S, A_ROT_DIM, (0, HALF))
    tab_b = _rope_tables(S, B_ROPE_DIM, (B_NOPE_DIM,))
    for i in range(depth):
        j = i // 2
        g = mix_norm[i][None, :]
        if i % 2 == 0:
            w_qkv, gq, gk = layers_a[j]
            q, k, v = _proj_a(x, g, w_qkv, gq, gk, tab_a)
            o = _attn_a(a_sink[j], q, k, v)
            w_o = a_w_o[j]
        else:
            w_in, w_uq, w_uk, w_uv, gql, gkvl, gq, gk = layers_b[j]
            q, k, v = _proj_b(x, g, w_in, gql, gkvl, w_uq, w_uk, w_uv, gq, gk, tab_b)
            o = _attn_b(q, k, v)
            w_o = b_w_o[j]
        x = _post(x.reshape(T, D_MODEL), o.reshape(T, -1), p, i, w_o, ffn_norm[i][None, :],
                  w_gu[i], w_d[i], ple_norm[i][None, :], w_pg[i], w_pp[i]).reshape(B, S, D_MODEL)
    return x


def kernel(x_prompt, x_sample, p_prompt, p_sample, mix_norm, a_w_qkv, a_q_norm, a_k_norm, a_sink, a_w_o, b_w_in, b_q_lat_norm, b_kv_lat_norm, b_w_uq, b_w_ukv, b_q_norm, b_k_norm, b_w_o, ffn_norm, ffn_w_gate_up, ffn_w_down, ple_norm, ple_w_gate, ple_w_proj):
    layers_a = [_prep_a(a_w_qkv[j], a_q_norm[j], a_k_norm[j]) for j in range(a_w_qkv.shape[0])]
    layers_b = []
    for j in range(b_w_in.shape[0]):
        w_in, w_uq, w_uk, w_uv, gq, gk = _prep_b(b_w_in[j], b_w_uq[j], b_w_ukv[j],
                                                  b_q_norm[j], b_k_norm[j])
        layers_b.append((w_in, w_uq, w_uk, w_uv, b_q_lat_norm[j][None, :],
                         b_kv_lat_norm[j][None, :], gq, gk))
    shared = (mix_norm, layers_a, layers_b, a_sink, a_w_o.astype(BF16), b_w_o.astype(BF16),
              ffn_norm, ffn_w_gate_up.astype(BF16), ffn_w_down.astype(BF16), ple_norm,
              ple_w_gate.astype(BF16), ple_w_proj.astype(BF16))
    return (_trunk(x_prompt, p_prompt, *shared), _trunk(x_sample, p_sample, *shared))
```

```python
import functools
import math

import jax
import jax.numpy as jnp
from jax import lax
from jax.experimental import pallas as pl
from jax.experimental.pallas import tpu as pltpu

F32 = jnp.float32
BF16 = jnp.bfloat16

D_MODEL = 1024
PLE_DIM = 256
RMS_EPS = 1e-6
ROPE_THETA = 500000.0
BLOCK = 128
A_HEADS = 16
A_KV_HEADS = 4
A_HEAD_DIM = 64
A_ROT_DIM = 16
B_HEADS = 16
B_Q_RANK = 384
B_KV_RANK = 128
B_NOPE_DIM = 64
B_ROPE_DIM = 32
B_V_DIM = 64
B_QK_DIM = B_NOPE_DIM + B_ROPE_DIM
FFN_HIDDEN = 2816

LANES = 128
HALF = LANES // 2
VMEM_LIMIT_BYTES = 56 * 1024 * 1024

LOG2E = math.log2(math.e)
MASKED = -1e30
NT_DIMS = (((1,), (1,)), ((), ()))

PROJ_TILE = 512
POST_TILE = 256
MLA_Q_TILE = 512
MLA_KV_TILE = 512
ATTN_A_BLOCKS_PER_STEP = 4


def _compiler_params(semantics):
    return pltpu.CompilerParams(dimension_semantics=semantics,
                                vmem_limit_bytes=VMEM_LIMIT_BYTES)


def _resident(shape):
    nd = len(shape)
    return pl.BlockSpec(shape, lambda *_: (0,) * nd, pipeline_mode=pl.Buffered(1))


def _rms(x, g):
    ms = jnp.mean(x * x, axis=-1, keepdims=True)
    return x * lax.rsqrt(ms + RMS_EPS) * g


def _sigmoid(x):
    return 1.0 / (1.0 + jnp.exp(-x))


def _lane_iota(shape):
    return lax.broadcasted_iota(jnp.int32, shape, len(shape) - 1)


def _rope(x, c, s_from_below, s_from_above, shift):
    return (x * c + pltpu.roll(x, shift, 1) * s_from_below
            + pltpu.roll(x, LANES - shift, 1) * s_from_above)


def _rope_tables(seq, rot_dim, lane_offsets):
    half = rot_dim // 2
    inv = 1.0 / (ROPE_THETA ** (jnp.arange(0, rot_dim, 2, dtype=F32) / rot_dim))
    ang = jnp.arange(seq, dtype=F32)[:, None] * inv[None, :]
    cos, sin = jnp.cos(ang), jnp.sin(ang)
    c = jnp.ones((seq, LANES), F32)
    lo = jnp.zeros((seq, LANES), F32)
    hi = jnp.zeros((seq, LANES), F32)
    for off in lane_offsets:
        c = c.at[:, off:off + half].set(cos).at[:, off + half:off + rot_dim].set(cos)
        hi = hi.at[:, off:off + half].set(-sin)
        lo = lo.at[:, off + half:off + rot_dim].set(sin)
    return c, lo, hi


def _ones_row(shape):
    return (lax.broadcasted_iota(jnp.int32, shape, 0) == B_V_DIM).astype(F32)


def _proj_a_kernel(x_ref, g_ref, wqk_ref, wvt_ref, gq_ref, gk_ref, c_ref, sl_ref, sh_ref,
                   q_ref, k_ref, vt_ref):
    x = x_ref[0]
    ts = x.shape[0]
    h = _rms(x, g_ref[...]).astype(BF16)
    qk = jnp.dot(h, wqk_ref[...], preferred_element_type=F32)
    vt = lax.dot_general(wvt_ref[...], h, NT_DIMS, preferred_element_type=F32)
    c, sl, sh = c_ref[...], sl_ref[...], sh_ref[...]
    is_lo = _lane_iota((ts, LANES)) < HALF
    shift = A_ROT_DIM // 2

    def head_norm(t, gain):
        sq = t * t
        ss_lo = jnp.sum(jnp.where(is_lo, sq, 0.0), axis=-1, keepdims=True)
        ss_hi = jnp.sum(jnp.where(is_lo, 0.0, sq), axis=-1, keepdims=True)
        r = jnp.where(is_lo, lax.rsqrt(ss_lo / A_HEAD_DIM + RMS_EPS),
                      lax.rsqrt(ss_hi / A_HEAD_DIM + RMS_EPS))
        return _rope(t * r * gain, c, sl, sh, shift)

    def split_pair(t, out_ref, j):
        out_ref[0, 2 * j] = jnp.where(is_lo, t, 0.0).astype(BF16)
        out_ref[0, 2 * j + 1] = jnp.where(is_lo, pltpu.roll(t, HALF, 1), 0.0).astype(BF16)

    n_q = A_HEADS * A_HEAD_DIM // LANES
    for j in range(n_q):
        split_pair(head_norm(qk[:, j * LANES:(j + 1) * LANES], gq_ref[...]), q_ref, j)
    k_off = n_q * LANES
    for j in range(A_KV_HEADS * A_HEAD_DIM // LANES):
        split_pair(head_norm(qk[:, k_off + j * LANES:k_off + (j + 1) * LANES], gk_ref[...]),
                   k_ref, j)
    ones = _ones_row((LANES, ts))
    for g in range(A_KV_HEADS):
        blk = (vt[g * LANES:(g + 1) * LANES, :] + ones).astype(BF16)
        for cc in range(ts // BLOCK):
            vt_ref[0, g, cc] = blk[:, cc * BLOCK:(cc + 1) * BLOCK]


def _proj_a(x, g, w_qk, w_vt, gq, gk, tables):
    B, S, _ = x.shape
    ts = min(PROJ_TILE, S)
    tab = pl.BlockSpec((ts, LANES), lambda b, s: (s, 0))
    consts = [g, w_qk, w_vt, gq, gk]
    return pl.pallas_call(
        _proj_a_kernel,
        grid=(B, S // ts),
        in_specs=[pl.BlockSpec((1, ts, D_MODEL), lambda b, s: (b, s, 0))]
                 + [_resident(a.shape) for a in consts] + [tab, tab, tab],
        out_specs=[pl.BlockSpec((1, A_HEADS, ts, LANES), lambda b, s: (b, 0, s, 0)),
                   pl.BlockSpec((1, A_KV_HEADS, ts, LANES), lambda b, s: (b, 0, s, 0)),
                   pl.BlockSpec((1, A_KV_HEADS, ts // BLOCK, LANES, BLOCK),
                                lambda b, s: (b, 0, s, 0, 0))],
        out_shape=[jax.ShapeDtypeStruct((B, A_HEADS, S, LANES), BF16),
                   jax.ShapeDtypeStruct((B, A_KV_HEADS, S, LANES), BF16),
                   jax.ShapeDtypeStruct((B, A_KV_HEADS, S // BLOCK, LANES, BLOCK), BF16)],
        compiler_params=_compiler_params(("parallel", "parallel")),
        name="proj_a",
    )(x, *consts, *tables)


def _attn_a_kernel(sink_ref, q_ref, k_ref, vt_ref, o_ref, *, blocks_per_step):
    step = pl.program_id(2)
    nb = vt_ref.shape[2]
    group = A_HEADS // A_KV_HEADS
    width = group * BLOCK
    key = lax.broadcasted_iota(jnp.int32, (BLOCK, width), 0)
    qry = lax.broadcasted_iota(jnp.int32, (BLOCK, width), 1) & (BLOCK - 1)
    sink = sink_ref[0] * LOG2E
    for r in range(blocks_per_step):
        n = step * blocks_per_step + r
        rows = slice(r * BLOCK, (r + 1) * BLOCK)
        q4 = q_ref[0, :, rows, :].reshape(width, LANES)
        i_prev = jnp.maximum(n - 1, 0)
        i_next = jnp.minimum(n + 1, nb - 1)

        def scores(i, q4=q4):
            k = k_ref[0, 0, pl.ds(pl.multiple_of(i * BLOCK, BLOCK), BLOCK), :]
            return lax.dot_general(k, q4, NT_DIMS, preferred_element_type=F32)

        s_p = jnp.where((key >= qry) & (n > 0), scores(i_prev), MASKED)
        s_c = scores(n)
        s_n = jnp.where((key <= qry) & (n < nb - 1), scores(i_next), MASKED)
        m = jnp.max(jnp.maximum(jnp.maximum(s_p, s_c), s_n), axis=0, keepdims=True)
        m = jnp.maximum(m, sink)
        acc = jnp.dot(vt_ref[0, 0, i_prev], jnp.exp2(s_p - m).astype(BF16),
                      preferred_element_type=F32)
        acc += jnp.dot(vt_ref[0, 0, n], jnp.exp2(s_c - m).astype(BF16),
                       preferred_element_type=F32)
        acc += jnp.dot(vt_ref[0, 0, i_next], jnp.exp2(s_n - m).astype(BF16),
                       preferred_element_type=F32)
        denom = acc[B_V_DIM:B_V_DIM + 1] + jnp.exp2(sink - m)
        o = acc[:B_V_DIM] / denom
        for pi in range(group // 2):
            pair = jnp.concatenate([o[:, (2 * pi) * BLOCK:(2 * pi + 1) * BLOCK],
                                    o[:, (2 * pi + 1) * BLOCK:(2 * pi + 2) * BLOCK]], axis=0)
            o_ref[0, rows, pi * LANES:(pi + 1) * LANES] = pair.T.astype(BF16)


def _attn_a(sink_rows, q, k, vt):
    B, _, S, _ = q.shape
    nb = S // BLOCK
    r = min(ATTN_A_BLOCKS_PER_STEP, nb)
    group = A_HEADS // A_KV_HEADS
    return pl.pallas_call(
        functools.partial(_attn_a_kernel, blocks_per_step=r),
        grid=(B, A_KV_HEADS, nb // r),
        in_specs=[pl.BlockSpec((1, 1, group * BLOCK), lambda b, g, n: (g, 0, 0)),
                  pl.BlockSpec((1, group, r * BLOCK, LANES), lambda b, g, n: (b, g, n, 0)),
                  pl.BlockSpec((1, 1, S, LANES), lambda b, g, n: (b, g, 0, 0)),
                  pl.BlockSpec((1, 1, nb, LANES, BLOCK), lambda b, g, n: (b, g, 0, 0, 0))],
        out_specs=pl.BlockSpec((1, r * BLOCK, group * A_HEAD_DIM), lambda b, g, n: (b, n, g)),
        out_shape=jax.ShapeDtypeStruct((B, S, A_HEADS * A_HEAD_DIM), BF16),
        compiler_params=_compiler_params(("parallel", "parallel", "arbitrary")),
        name="attn_a",
    )(sink_rows, q, k, vt)


def _proj_b_kernel(x_ref, g_ref, win_ref, gql_ref, gkvl_ref, wuq_ref, wuk_ref, wuvt_ref,
                   gq_ref, gk_ref, c_ref, sl_ref, sh_ref, q_ref, k_ref, vt_ref):
    x = x_ref[0]
    ts = x.shape[0]
    h = _rms(x, g_ref[...]).astype(BF16)
    lat = jnp.dot(h, win_ref[...], preferred_element_type=F32)
    cq = _rms(lat[:, :B_Q_RANK], gql_ref[...]).astype(BF16)
    ckv = _rms(lat[:, B_Q_RANK:B_Q_RANK + B_KV_RANK], gkvl_ref[...]).astype(BF16)
    k_rope = lat[:, B_Q_RANK + B_KV_RANK:]
    q_all = jnp.dot(cq, wuq_ref[...], preferred_element_type=F32)
    kn_all = jnp.dot(ckv, wuk_ref[...], preferred_element_type=F32)
    vt_all = lax.dot_general(wuvt_ref[...], ckv, NT_DIMS,
                             preferred_element_type=F32)
    c, sl, sh = c_ref[...], sl_ref[...], sh_ref[...]
    shift = B_ROPE_DIM // 2
    ones = _ones_row((LANES, ts))
    tk = vt_ref.shape[4]

    def head_norm(t, gain):
        ms = jnp.sum(t * t, axis=-1, keepdims=True) / B_QK_DIM
        return _rope(t * lax.rsqrt(ms + RMS_EPS) * gain, c, sl, sh, shift)

    for hd in range(B_HEADS):
        sl_h = slice(hd * LANES, (hd + 1) * LANES)
        q_ref[0, hd] = head_norm(q_all[:, sl_h], gq_ref[...]).astype(BF16)
        k_ref[0, hd] = head_norm(kn_all[:, sl_h] + k_rope, gk_ref[...]).astype(BF16)
        blk = (vt_all[sl_h, :] + ones).astype(BF16)
        for cc in range(ts // tk):
            vt_ref[0, hd, cc] = blk[:, cc * tk:(cc + 1) * tk]


def _proj_b(x, g, w_in, gql, gkvl, w_uq, w_uk, w_uvt, gq, gk, tables):
    B, S, _ = x.shape
    ts = min(PROJ_TILE, S)
    tk = min(MLA_KV_TILE, S // 2)
    tab = pl.BlockSpec((ts, LANES), lambda b, s: (s, 0))
    out = pl.BlockSpec((1, B_HEADS, ts, LANES), lambda b, s: (b, 0, s, 0))
    shp = jax.ShapeDtypeStruct((B, B_HEADS, S, LANES), BF16)
    consts = [g, w_in, gql, gkvl, w_uq, w_uk, w_uvt, gq, gk]
    return pl.pallas_call(
        _proj_b_kernel,
        grid=(B, S // ts),
        in_specs=[pl.BlockSpec((1, ts, D_MODEL), lambda b, s: (b, s, 0))]
                 + [_resident(a.shape) for a in consts] + [tab, tab, tab],
        out_specs=[out, out,
                   pl.BlockSpec((1, B_HEADS, ts // tk, LANES, tk), lambda b, s: (b, 0, s, 0, 0))],
        out_shape=[shp, shp, jax.ShapeDtypeStruct((B, B_HEADS, S // tk, LANES, tk), BF16)],
        compiler_params=_compiler_params(("parallel", "parallel")),
        name="proj_b",
    )(x, *consts, *tables)


def _attn_b_kernel(q_ref, k_ref, vt_ref, o_ref, s0_ref, s1_ref, m_ref, acc_ref):
    n_chunks, tk = vt_ref.shape[2], vt_ref.shape[4]
    m_ref[...] = jnp.full(m_ref.shape, MASKED, F32)
    acc_ref[...] = jnp.zeros(acc_ref.shape, F32)

    def scores(c, s_ref):
        start = pl.multiple_of(c * tk, tk)
        col_max = []
        for hh in range(2):
            s = lax.dot_general(k_ref[0, hh, pl.ds(start, tk), :], q_ref[0, hh], NT_DIMS,
                                preferred_element_type=F32)
            s_ref[hh] = s
            col_max.append(jnp.max(s, axis=0, keepdims=True))
        return tuple(col_max)

    def accumulate(c, s_ref, col_max):
        for hh in range(2):
            m_old = m_ref[hh]
            m_new = jnp.maximum(m_old, col_max[hh])
            p = jnp.exp2(s_ref[hh] - m_new).astype(BF16)
            acc_ref[hh] = (acc_ref[hh] * jnp.exp2(m_old - m_new)
                           + jnp.dot(vt_ref[0, hh, c], p, preferred_element_type=F32))
            m_ref[hh] = m_new

    def pair(i, max0):
        c = 2 * i
        max1 = scores(c + 1, s1_ref)
        accumulate(c, s0_ref, max0)
        max0 = scores(c + 2, s0_ref)
        accumulate(c + 1, s1_ref, max1)
        return max0

    max0 = lax.fori_loop(0, n_chunks // 2 - 1, pair, scores(0, s0_ref))
    max1 = scores(n_chunks - 1, s1_ref)
    accumulate(n_chunks - 2, s0_ref, max0)
    accumulate(n_chunks - 1, s1_ref, max1)
    halves = [acc_ref[hh][:B_V_DIM] / acc_ref[hh][B_V_DIM:B_V_DIM + 1] for hh in range(2)]
    o_ref[0] = jnp.concatenate(halves, axis=0).T.astype(BF16)


def _attn_b(q, k, vt):
    B, H, S, _ = q.shape
    tq = min(MLA_Q_TILE, S)
    n_chunks, tk = vt.shape[2], vt.shape[4]
    assert n_chunks % 2 == 0
    score_buf = pltpu.VMEM((2, tk, tq), F32)
    return pl.pallas_call(
        _attn_b_kernel,
        grid=(B, H // 2, S // tq),
        in_specs=[pl.BlockSpec((1, 2, tq, LANES), lambda b, j, i: (b, j, i, 0)),
                  pl.BlockSpec((1, 2, S, LANES), lambda b, j, i: (b, j, 0, 0)),
                  pl.BlockSpec((1, 2, n_chunks, LANES, tk), lambda b, j, i: (b, j, 0, 0, 0))],
        out_specs=pl.BlockSpec((1, tq, LANES), lambda b, j, i: (b, i, j)),
        out_shape=jax.ShapeDtypeStruct((B, S, H * B_V_DIM), BF16),
        scratch_shapes=[score_buf, score_buf,
                        pltpu.VMEM((2, 1, tq), F32), pltpu.VMEM((2, LANES, tq), F32)],
        compiler_params=_compiler_params(("parallel", "parallel", "arbitrary")),
        name="attn_b",
    )(q, k, vt)


def _post_kernel(x_ref, o_ref, p_ref, wo_ref, fg_ref, wgu_ref, wd_ref, pg_ref, wpg_ref,
                 wpp_ref, out_ref):
    x = x_ref[...] + jnp.dot(o_ref[...], wo_ref[...], preferred_element_type=F32)
    h = _rms(x, fg_ref[...]).astype(BF16)
    gu = jnp.dot(h, wgu_ref[...], preferred_element_type=F32)
    gate, up = gu[:, :FFN_HIDDEN], gu[:, FFN_HIDDEN:]
    act = (gate * _sigmoid(gate) * up).astype(BF16)
    x = x + jnp.dot(act, wd_ref[...], preferred_element_type=F32)
    h = _rms(x, pg_ref[...]).astype(BF16)
    ple_gate = _sigmoid(jnp.dot(h, wpg_ref[...], preferred_element_type=F32))
    emb = jnp.dot(p_ref[0].astype(BF16), wpp_ref[...], preferred_element_type=F32)
    out_ref[...] = x + ple_gate * emb


def _post(x, o, p, layer, wo, fg, wgu, wd, pg, wpg, wpp):
    T = x.shape[0]
    tm = min(POST_TILE, T)
    consts = [wo, fg, wgu, wd, pg, wpg, wpp]
    row = lambda width: pl.BlockSpec((tm, width), lambda t: (t, 0))
    return pl.pallas_call(
        _post_kernel,
        grid=(T // tm,),
        in_specs=[row(D_MODEL), row(o.shape[1]),
                  pl.BlockSpec((1, tm, PLE_DIM), lambda t: (layer, t, 0))]
                 + [_resident(a.shape) for a in consts],
        out_specs=row(D_MODEL),
        out_shape=jax.ShapeDtypeStruct((T, D_MODEL), F32),
        compiler_params=_compiler_params(("parallel",)),
        name="post",
    )(x, o, p, *consts)


def _head_slots(w, heads, width):
    kdim = w.shape[0]
    w = w.reshape(kdim, heads, width)
    return jnp.pad(w, ((0, 0), (0, 0), (0, LANES - width))).reshape(kdim, heads * LANES)


def _prep_a(w_qkv, q_gain, k_gain, sink):
    scale = (A_HEAD_DIM ** -0.5) * LOG2E
    n_qk = (A_HEADS + A_KV_HEADS) * A_HEAD_DIM
    gq = jnp.tile(q_gain, 2)[None, :] * scale
    gk = jnp.tile(k_gain, 2)[None, :]
    w_vt = _head_slots(w_qkv[:, n_qk:], A_KV_HEADS, A_HEAD_DIM).T.astype(BF16)
    group = A_HEADS // A_KV_HEADS
    sink_rows = jnp.repeat(sink.reshape(A_KV_HEADS, group), BLOCK, axis=1)[:, None, :]
    return w_qkv[:, :n_qk].astype(BF16), w_vt, gq, gk, sink_rows


def _prep_b(w_in, w_uq, w_ukv, q_gain, k_gain):
    scale = (B_QK_DIM ** -0.5) * LOG2E
    rope_cols = w_in[:, B_Q_RANK + B_KV_RANK:]
    rope_slot = jnp.pad(rope_cols, ((0, 0), (B_NOPE_DIM, LANES - B_QK_DIM)))
    w_in_p = jnp.concatenate([w_in[:, :B_Q_RANK + B_KV_RANK], rope_slot], axis=1).astype(BF16)
    w_uq_p = _head_slots(w_uq, B_HEADS, B_QK_DIM).astype(BF16)
    ukv = w_ukv.reshape(B_KV_RANK, B_HEADS, B_NOPE_DIM + B_V_DIM)
    w_uk_p = _head_slots(ukv[:, :, :B_NOPE_DIM].reshape(B_KV_RANK, -1), B_HEADS, B_NOPE_DIM).astype(BF16)
    w_uvt_p = _head_slots(ukv[:, :, B_NOPE_DIM:].reshape(B_KV_RANK, -1), B_HEADS, B_V_DIM).T.astype(BF16)
    pad = (0, LANES - B_QK_DIM)
    gq = jnp.pad(q_gain, pad)[None, :] * scale
    gk = jnp.pad(k_gain, pad)[None, :]
    return w_in_p, w_uq_p, w_uk_p, w_uvt_p, gq, gk


def _trunk(x, p, mix_norm, layers_a, layers_b, a_w_o, b_w_o, ffn_norm, w_gu, w_d,
           ple_norm, w_pg, w_pp):
    B, S, _ = x.shape
    depth = p.shape[0]
    T = B * S
    p = p.reshape(depth, T, PLE_DIM)
    tab_a = _rope_tables(S, A_ROT_DIM, (0, HALF))
    tab_b = _rope_tables(S, B_ROPE_DIM, (B_NOPE_DIM,))
    for i in range(depth):
        j = i // 2
        g = mix_norm[i][None, :]
        if i % 2 == 0:
            w_qk, w_vt, gq, gk, sink_rows = layers_a[j]
            q, k, vt = _proj_a(x, g, w_qk, w_vt, gq, gk, tab_a)
            o = _attn_a(sink_rows, q, k, vt)
            w_o = a_w_o[j]
        else:
            w_in, w_uq, w_uk, w_uvt, gql, gkvl, gq, gk = layers_b[j]
            q, k, vt = _proj_b(x, g, w_in, gql, gkvl, w_uq, w_uk, w_uvt, gq, gk, tab_b)
            o = _attn_b(q, k, vt)
            w_o = b_w_o[j]
        x = _post(x.reshape(T, D_MODEL), o.reshape(T, -1), p, i, w_o, ffn_norm[i][None, :],
                  w_gu[i], w_d[i], ple_norm[i][None, :], w_pg[i], w_pp[i]).reshape(B, S, D_MODEL)
    return x


def kernel(x_prompt, x_sample, p_prompt, p_sample, mix_norm, a_w_qkv, a_q_norm, a_k_norm, a_sink, a_w_o, b_w_in, b_q_lat_norm, b_kv_lat_norm, b_w_uq, b_w_ukv, b_q_norm, b_k_norm, b_w_o, ffn_norm, ffn_w_gate_up, ffn_w_down, ple_norm, ple_w_gate, ple_w_proj):
    layers_a = [_prep_a(a_w_qkv[j], a_q_norm[j], a_k_norm[j], a_sink[j])
                for j in range(a_w_qkv.shape[0])]
    layers_b = []
    for j in range(b_w_in.shape[0]):
        w_in, w_uq, w_uk, w_uvt, gq, gk = _prep_b(b_w_in[j], b_w_uq[j], b_w_ukv[j],
                                                   b_q_norm[j], b_k_norm[j])
        layers_b.append((w_in, w_uq, w_uk, w_uvt, b_q_lat_norm[j][None, :],
                         b_kv_lat_norm[j][None, :], gq, gk))
    shared = (mix_norm, layers_a, layers_b, a_w_o.astype(BF16), b_w_o.astype(BF16),
              ffn_norm, ffn_w_gate_up.astype(BF16), ffn_w_down.astype(BF16), ple_norm,
              ple_w_gate.astype(BF16), ple_w_proj.astype(BF16))
    return (_trunk(x_prompt, p_prompt, *shared), _trunk(x_sample, p_sample, *shared))
```

```python
import functools
import math

import jax
import jax.numpy as jnp
from jax import lax
from jax.experimental import pallas as pl
from jax.experimental.pallas import tpu as pltpu

F32 = jnp.float32
BF16 = jnp.bfloat16

D_MODEL = 1024
PLE_DIM = 256
RMS_EPS = 1e-6
ROPE_THETA = 500000.0
BLOCK = 128
A_HEADS = 16
A_KV_HEADS = 4
A_HEAD_DIM = 64
A_ROT_DIM = 16
B_HEADS = 16
B_Q_RANK = 384
B_KV_RANK = 128
B_NOPE_DIM = 64
B_ROPE_DIM = 32
B_V_DIM = 64
B_QK_DIM = B_NOPE_DIM + B_ROPE_DIM
FFN_HIDDEN = 2816

LANES = 128
HALF = LANES // 2
VMEM_LIMIT_BYTES = 56 * 1024 * 1024

LOG2E = math.log2(math.e)
MASKED = -1e30
NT_DIMS = (((1,), (1,)), ((), ()))

PROJ_TILE = 512
POST_TILE = 512
MLA_Q_TILE = 512
MLA_KV_TILE = 512
MLA_WINDOW_CHUNKS = 4
MLA_VT_ROWS = 80
ATTN_A_BLOCKS_PER_STEP = 8


def _compiler_params(semantics):
    return pltpu.CompilerParams(dimension_semantics=semantics,
                                vmem_limit_bytes=VMEM_LIMIT_BYTES)


def _resident(shape):
    nd = len(shape)
    return pl.BlockSpec(shape, lambda *_: (0,) * nd, pipeline_mode=pl.Buffered(1))


def _rms(x, g):
    ms = jnp.mean(x * x, axis=-1, keepdims=True)
    return x * lax.rsqrt(ms + RMS_EPS) * g


def _sigmoid(x):
    return 1.0 / (1.0 + jnp.exp(-x))


def _lane_iota(shape):
    return lax.broadcasted_iota(jnp.int32, shape, len(shape) - 1)


def _across_lanes(col_block, n):
    return jnp.concatenate([col_block] * (n // LANES), axis=1)


def _rope_rows(x1, x2, cos_t, sin_t):
    return x1 * cos_t - x2 * sin_t, x1 * sin_t + x2 * cos_t


def _rope_lanes(x, c, s_from_below, s_from_above, shift):
    return (x * c + pltpu.roll(x, shift, 1) * s_from_below
            + pltpu.roll(x, LANES - shift, 1) * s_from_above)


def _angles(seq, rot_dim):
    inv = 1.0 / (ROPE_THETA ** (jnp.arange(0, rot_dim, 2, dtype=F32) / rot_dim))
    ang = jnp.arange(seq, dtype=F32)[:, None] * inv[None, :]
    return jnp.cos(ang), jnp.sin(ang)


def _lane_tables(seq, rot_dim, lane_offsets):
    half = rot_dim // 2
    cos, sin = _angles(seq, rot_dim)
    c = jnp.ones((seq, LANES), F32)
    lo = jnp.zeros((seq, LANES), F32)
    hi = jnp.zeros((seq, LANES), F32)
    for off in lane_offsets:
        c = c.at[:, off:off + half].set(cos).at[:, off + half:off + rot_dim].set(cos)
        hi = hi.at[:, off:off + half].set(-sin)
        lo = lo.at[:, off + half:off + rot_dim].set(sin)
    return c, lo, hi


def _row_tables(seq, rot_dim):
    cos, sin = _angles(seq, rot_dim)
    return cos.T, sin.T


def _ones_row(shape):
    return (lax.broadcasted_iota(jnp.int32, shape, 0) == B_V_DIM).astype(F32)


def _proj_a_kernel(x_ref, g_ref, wqt_ref, wk_ref, wvt_ref, gq_ref, gk_ref, ct_ref, st_ref,
                   c_ref, sl_ref, sh_ref, q_ref, k_ref, vt_ref):
    x = x_ref[0]
    ts = x.shape[0]
    h = _rms(x, g_ref[...]).astype(BF16)
    q_t = lax.dot_general(wqt_ref[...], h, NT_DIMS, preferred_element_type=F32)
    kk = jnp.dot(h, wk_ref[...], preferred_element_type=F32)
    vt = lax.dot_general(wvt_ref[...], h, NT_DIMS, preferred_element_type=F32)

    gq = _across_lanes(gq_ref[...], ts)
    cos_t, sin_t = ct_ref[...], st_ref[...]
    half = A_ROT_DIM // 2
    pad = jnp.zeros((LANES - A_HEAD_DIM, ts), BF16)
    for hd in range(A_HEADS):
        t = q_t[hd * A_HEAD_DIM:(hd + 1) * A_HEAD_DIM, :]
        ms = jnp.sum(t * t, axis=0, keepdims=True) / A_HEAD_DIM
        tn = t * lax.rsqrt(ms + RMS_EPS) * gq
        o1, o2 = _rope_rows(tn[:half], tn[half:A_ROT_DIM], cos_t, sin_t)
        out = jnp.concatenate([o1, o2, tn[A_ROT_DIM:]], axis=0)
        q_ref[0, hd, :A_HEAD_DIM, :] = out.astype(BF16)
        q_ref[0, hd, A_HEAD_DIM:, :] = pad

    c, sl, sh = c_ref[...], sl_ref[...], sh_ref[...]
    is_lo = _lane_iota((ts, LANES)) < HALF
    for j in range(A_KV_HEADS * A_HEAD_DIM // LANES):
        t = kk[:, j * LANES:(j + 1) * LANES]
        sq = t * t
        ss_lo = jnp.sum(jnp.where(is_lo, sq, 0.0), axis=-1, keepdims=True)
        ss_hi = jnp.sum(jnp.where(is_lo, 0.0, sq), axis=-1, keepdims=True)
        r = jnp.where(is_lo, lax.rsqrt(ss_lo / A_HEAD_DIM + RMS_EPS),
                      lax.rsqrt(ss_hi / A_HEAD_DIM + RMS_EPS))
        t = _rope_lanes(t * r * gk_ref[...], c, sl, sh, half)
        k_ref[0, 2 * j] = jnp.where(is_lo, t, 0.0).astype(BF16)
        k_ref[0, 2 * j + 1] = jnp.where(is_lo, pltpu.roll(t, HALF, 1), 0.0).astype(BF16)

    ones = _ones_row((LANES, ts))
    for g in range(A_KV_HEADS):
        blk = (vt[g * LANES:(g + 1) * LANES, :] + ones).astype(BF16)
        for cc in range(ts // BLOCK):
            vt_ref[0, g, cc] = blk[:, cc * BLOCK:(cc + 1) * BLOCK]


def _proj_a(x, g, w_qt, w_k, w_vt, gq, gk, row_tabs, lane_tabs):
    B, S, _ = x.shape
    ts = min(PROJ_TILE, S)
    rtab = pl.BlockSpec((A_ROT_DIM // 2, ts), lambda b, s: (0, s))
    ltab = pl.BlockSpec((ts, LANES), lambda b, s: (s, 0))
    consts = [g, w_qt, w_k, w_vt, gq, gk]
    return pl.pallas_call(
        _proj_a_kernel,
        grid=(B, S // ts),
        in_specs=[pl.BlockSpec((1, ts, D_MODEL), lambda b, s: (b, s, 0))]
                 + [_resident(a.shape) for a in consts] + [rtab, rtab, ltab, ltab, ltab],
        out_specs=[pl.BlockSpec((1, A_HEADS, LANES, ts), lambda b, s: (b, 0, 0, s)),
                   pl.BlockSpec((1, A_KV_HEADS, ts, LANES), lambda b, s: (b, 0, s, 0)),
                   pl.BlockSpec((1, A_KV_HEADS, ts // BLOCK, LANES, BLOCK),
                                lambda b, s: (b, 0, s, 0, 0))],
        out_shape=[jax.ShapeDtypeStruct((B, A_HEADS, LANES, S), BF16),
                   jax.ShapeDtypeStruct((B, A_KV_HEADS, S, LANES), BF16),
                   jax.ShapeDtypeStruct((B, A_KV_HEADS, S // BLOCK, LANES, BLOCK), BF16)],
        compiler_params=_compiler_params(("parallel", "parallel")),
        name="proj_a",
    )(x, *consts, *row_tabs, *lane_tabs)


def _attn_a_kernel(sink_ref, qt_ref, k_ref, vt_ref, o_ref, *, blocks_per_step):
    step = pl.program_id(2)
    nb = vt_ref.shape[2]
    group = A_HEADS // A_KV_HEADS
    width = group * BLOCK
    key = lax.broadcasted_iota(jnp.int32, (BLOCK, width), 0)
    qry = lax.broadcasted_iota(jnp.int32, (BLOCK, width), 1) & (BLOCK - 1)
    sink = sink_ref[0] * LOG2E
    for r in range(blocks_per_step):
        n = step * blocks_per_step + r
        rows = slice(r * BLOCK, (r + 1) * BLOCK)
        q4t = jnp.concatenate([qt_ref[0, hd, :, rows] for hd in range(group)], axis=1)
        i_prev = jnp.maximum(n - 1, 0)
        i_next = jnp.minimum(n + 1, nb - 1)

        def scores(i, q4t=q4t):
            k = k_ref[0, 0, pl.ds(pl.multiple_of(i * BLOCK, BLOCK), BLOCK), :]
            return jnp.dot(k, q4t, preferred_element_type=F32)

        s_p = jnp.where((key >= qry) & (n > 0), scores(i_prev), MASKED)
        s_c = scores(n)
        s_n = jnp.where((key <= qry) & (n < nb - 1), scores(i_next), MASKED)
        m = jnp.max(jnp.maximum(jnp.maximum(s_p, s_c), s_n), axis=0, keepdims=True)
        m = jnp.maximum(m, sink)
        acc = jnp.dot(vt_ref[0, 0, i_prev], jnp.exp2(s_p - m).astype(BF16),
                      preferred_element_type=F32)
        acc += jnp.dot(vt_ref[0, 0, n], jnp.exp2(s_c - m).astype(BF16),
                       preferred_element_type=F32)
        acc += jnp.dot(vt_ref[0, 0, i_next], jnp.exp2(s_n - m).astype(BF16),
                       preferred_element_type=F32)
        denom = acc[B_V_DIM:B_V_DIM + 1] + jnp.exp2(sink - m)
        o = acc[:B_V_DIM] / denom
        for pi in range(group // 2):
            pair = jnp.concatenate([o[:, (2 * pi) * BLOCK:(2 * pi + 1) * BLOCK],
                                    o[:, (2 * pi + 1) * BLOCK:(2 * pi + 2) * BLOCK]], axis=0)
            o_ref[0, rows, pi * LANES:(pi + 1) * LANES] = pair.T.astype(BF16)


def _attn_a(sink_rows, qt, k, vt):
    B, _, _, S = qt.shape
    nb = S // BLOCK
    r = min(ATTN_A_BLOCKS_PER_STEP, nb)
    group = A_HEADS // A_KV_HEADS
    return pl.pallas_call(
        functools.partial(_attn_a_kernel, blocks_per_step=r),
        grid=(B, A_KV_HEADS, nb // r),
        in_specs=[pl.BlockSpec((1, 1, group * BLOCK), lambda b, g, n: (g, 0, 0)),
                  pl.BlockSpec((1, group, LANES, r * BLOCK), lambda b, g, n: (b, g, 0, n)),
                  pl.BlockSpec((1, 1, S, LANES), lambda b, g, n: (b, g, 0, 0)),
                  pl.BlockSpec((1, 1, nb, LANES, BLOCK), lambda b, g, n: (b, g, 0, 0, 0))],
        out_specs=pl.BlockSpec((1, r * BLOCK, group * A_HEAD_DIM), lambda b, g, n: (b, n, g)),
        out_shape=jax.ShapeDtypeStruct((B, S, A_HEADS * A_HEAD_DIM), BF16),
        compiler_params=_compiler_params(("parallel", "parallel", "arbitrary")),
        name="attn_a",
    )(sink_rows, qt, k, vt)


def _proj_b_kernel(x_ref, g_ref, win_ref, gql_ref, gkvl_ref, wuqt_ref, wuk_ref, wuvt_ref,
                   gq_ref, gk_ref, gksw_ref, ct_ref, st_ref, c_ref, ssg_ref,
                   q_ref, k_ref, vt_ref):
    x = x_ref[0]
    ts = x.shape[0]
    h = _rms(x, g_ref[...]).astype(BF16)
    lat = jnp.dot(h, win_ref[...], preferred_element_type=F32)
    cq = _rms(lat[:, :B_Q_RANK], gql_ref[...]).astype(BF16)
    ckv = _rms(lat[:, B_Q_RANK:B_Q_RANK + B_KV_RANK], gkvl_ref[...]).astype(BF16)
    rope_off = B_Q_RANK + B_KV_RANK
    k_rope = lat[:, rope_off:rope_off + LANES]
    k_rope_sw = lat[:, rope_off + LANES:]
    q_t = lax.dot_general(wuqt_ref[...], cq, NT_DIMS, preferred_element_type=F32)
    kn_all = jnp.dot(ckv, wuk_ref[...], preferred_element_type=F32)
    vt_all = lax.dot_general(wuvt_ref[...], ckv, NT_DIMS, preferred_element_type=F32)

    gq = _across_lanes(gq_ref[...], ts)
    cos_t, sin_t = ct_ref[...], st_ref[...]
    half = B_ROPE_DIM // 2
    gain_cos = c_ref[...] * gk_ref[...]
    rope_term = k_rope_sw * (ssg_ref[...] * gksw_ref[...])
    ones = _ones_row((MLA_VT_ROWS, ts))
    tk = vt_ref.shape[4]
    for hd in range(B_HEADS):
        sl_h = slice(hd * LANES, (hd + 1) * LANES)
        t = q_t[sl_h, :]
        ms = jnp.sum(t * t, axis=0, keepdims=True) / B_QK_DIM
        tn = t * lax.rsqrt(ms + RMS_EPS) * gq
        o1, o2 = _rope_rows(tn[B_NOPE_DIM:B_NOPE_DIM + half], tn[B_NOPE_DIM + half:B_QK_DIM],
                            cos_t, sin_t)
        q_ref[0, hd] = jnp.concatenate([tn[:B_NOPE_DIM], o1, o2, tn[B_QK_DIM:]],
                                       axis=0).astype(BF16)
        xk = kn_all[:, sl_h] + k_rope
        msk = jnp.sum(xk * xk, axis=-1, keepdims=True) / B_QK_DIM
        k_ref[0, hd] = ((xk * gain_cos + rope_term) * lax.rsqrt(msk + RMS_EPS)).astype(BF16)
        blk = (vt_all[hd * MLA_VT_ROWS:(hd + 1) * MLA_VT_ROWS, :] + ones).astype(BF16)
        for cc in range(ts // tk):
            vt_ref[0, hd, cc] = blk[:, cc * tk:(cc + 1) * tk]


def _proj_b(x, consts, row_tabs, lane_tabs):
    B, S, _ = x.shape
    ts = min(PROJ_TILE, S)
    tk = min(MLA_KV_TILE, S // 2)
    rtab = pl.BlockSpec((B_ROPE_DIM // 2, ts), lambda b, s: (0, s))
    ltab = pl.BlockSpec((ts, LANES), lambda b, s: (s, 0))
    return pl.pallas_call(
        _proj_b_kernel,
        grid=(B, S // ts),
        in_specs=[pl.BlockSpec((1, ts, D_MODEL), lambda b, s: (b, s, 0))]
                 + [_resident(a.shape) for a in consts] + [rtab, rtab, ltab, ltab],
        out_specs=[pl.BlockSpec((1, B_HEADS, LANES, ts), lambda b, s: (b, 0, 0, s)),
                   pl.BlockSpec((1, B_HEADS, ts, LANES), lambda b, s: (b, 0, s, 0)),
                   pl.BlockSpec((1, B_HEADS, ts // tk, MLA_VT_ROWS, tk),
                                lambda b, s: (b, 0, s, 0, 0))],
        out_shape=[jax.ShapeDtypeStruct((B, B_HEADS, LANES, S), BF16),
                   jax.ShapeDtypeStruct((B, B_HEADS, S, LANES), BF16),
                   jax.ShapeDtypeStruct((B, B_HEADS, S // tk, MLA_VT_ROWS, tk), BF16)],
        compiler_params=_compiler_params(("parallel", "parallel")),
        name="proj_b",
    )(x, *consts, *row_tabs, *lane_tabs)


def _attn_b_kernel(qt_ref, k_ref, vt_ref, o_ref, s0_ref, s1_ref, m_ref, acc_ref, *,
                   window_chunks):
    n_chunks, tk = vt_ref.shape[2], vt_ref.shape[4]
    m_ref[...] = jnp.full(m_ref.shape, MASKED, F32)
    acc_ref[...] = jnp.zeros(acc_ref.shape, F32)

    def scores(c, s_ref):
        start = pl.multiple_of(c * tk, tk)
        col_max = []
        for hh in range(2):
            s = jnp.dot(k_ref[0, hh, pl.ds(start, tk), :], qt_ref[0, hh],
                        preferred_element_type=F32)
            s_ref[hh] = s
            col_max.append(jnp.max(s, axis=0, keepdims=True))
        return tuple(col_max)

    def accumulate(c, s_ref, col_max):
        for hh in range(2):
            m_old = m_ref[hh]
            m_new = jnp.maximum(m_old, col_max[hh])
            p = jnp.exp2(s_ref[hh] - m_new).astype(BF16)
            acc_ref[hh] = (acc_ref[hh] * jnp.exp2(m_old - m_new)
                           + jnp.dot(vt_ref[0, hh, c], p, preferred_element_type=F32))
            m_ref[hh] = m_new

    bufs = (s0_ref, s1_ref)

    def window(c0, col_max, last):
        for w in range(window_chunks):
            is_final = last and w == window_chunks - 1
            nxt = None if is_final else scores(c0 + w + 1, bufs[(w + 1) % 2])
            accumulate(c0 + w, bufs[w % 2], col_max)
            col_max = nxt
        return col_max

    n_windows = n_chunks // window_chunks
    col_max = lax.fori_loop(0, n_windows - 1,
                            lambda i, mx: window(i * window_chunks, mx, False),
                            scores(0, s0_ref))
    window((n_windows - 1) * window_chunks, col_max, True)
    halves = [acc_ref[hh][:B_V_DIM] / acc_ref[hh][B_V_DIM:B_V_DIM + 1] for hh in range(2)]
    o_ref[0] = jnp.concatenate(halves, axis=0).T.astype(BF16)


def _attn_b(qt, k, vt):
    B, H, _, S = qt.shape
    tq = min(MLA_Q_TILE, S)
    n_chunks, tk = vt.shape[2], vt.shape[4]
    window = MLA_WINDOW_CHUNKS if n_chunks % MLA_WINDOW_CHUNKS == 0 else 2
    assert n_chunks % window == 0
    score_buf = pltpu.VMEM((2, tk, tq), F32)
    return pl.pallas_call(
        functools.partial(_attn_b_kernel, window_chunks=window),
        grid=(B, H // 2, S // tq),
        in_specs=[pl.BlockSpec((1, 2, LANES, tq), lambda b, j, i: (b, j, 0, i)),
                  pl.BlockSpec((1, 2, S, LANES), lambda b, j, i: (b, j, 0, 0)),
                  pl.BlockSpec((1, 2, n_chunks, MLA_VT_ROWS, tk), lambda b, j, i: (b, j, 0, 0, 0))],
        out_specs=pl.BlockSpec((1, tq, LANES), lambda b, j, i: (b, i, j)),
        out_shape=jax.ShapeDtypeStruct((B, S, H * B_V_DIM), BF16),
        scratch_shapes=[score_buf, score_buf, pltpu.VMEM((2, 1, tq), F32),
                        pltpu.VMEM((2, MLA_VT_ROWS, tq), F32)],
        compiler_params=_compiler_params(("parallel", "parallel", "arbitrary")),
        name="attn_b",
    )(qt, k, vt)


def _post_kernel(x_ref, o_ref, p_ref, wo_ref, fg_ref, wgu_ref, wd_ref, pg_ref, wpg_ref,
                 wpp_ref, out_ref):
    x = x_ref[...] + jnp.dot(o_ref[...], wo_ref[...], preferred_element_type=F32)
    h = _rms(x, fg_ref[...]).astype(BF16)
    gu = jnp.dot(h, wgu_ref[...], preferred_element_type=F32)
    gate, up = gu[:, :FFN_HIDDEN], gu[:, FFN_HIDDEN:]
    act = (gate * _sigmoid(gate) * up).astype(BF16)
    x = x + jnp.dot(act, wd_ref[...], preferred_element_type=F32)
    h = _rms(x, pg_ref[...]).astype(BF16)
    ple_gate = _sigmoid(jnp.dot(h, wpg_ref[...], preferred_element_type=F32))
    emb = jnp.dot(p_ref[0].astype(BF16), wpp_ref[...], preferred_element_type=F32)
    out_ref[...] = x + ple_gate * emb


def _post(x, o, p, layer, wo, fg, wgu, wd, pg, wpg, wpp):
    T = x.shape[0]
    tm = min(POST_TILE, T)
    consts = [wo, fg, wgu, wd, pg, wpg, wpp]
    row = lambda width: pl.BlockSpec((tm, width), lambda t: (t, 0))
    return pl.pallas_call(
        _post_kernel,
        grid=(T // tm,),
        in_specs=[row(D_MODEL), row(o.shape[1]),
                  pl.BlockSpec((1, tm, PLE_DIM), lambda t: (layer, t, 0))]
                 + [_resident(a.shape) for a in consts],
        out_specs=row(D_MODEL),
        out_shape=jax.ShapeDtypeStruct((T, D_MODEL), F32),
        compiler_params=_compiler_params(("parallel",)),
        name="post",
    )(x, o, p, *consts)


def _head_slots(w, heads, width):
    kdim = w.shape[0]
    w = w.reshape(kdim, heads, width)
    return jnp.pad(w, ((0, 0), (0, 0), (0, LANES - width))).reshape(kdim, heads * LANES)


def _lane_replicated(v):
    return jnp.broadcast_to(v[:, None], (v.shape[0], LANES))


def _prep_a(w_qkv, q_gain, k_gain, sink):
    scale = (A_HEAD_DIM ** -0.5) * LOG2E
    n_q = A_HEADS * A_HEAD_DIM
    n_qk = n_q + A_KV_HEADS * A_HEAD_DIM
    w_qt = w_qkv[:, :n_q].T.astype(BF16)
    w_k = w_qkv[:, n_q:n_qk].astype(BF16)
    w_vt = _head_slots(w_qkv[:, n_qk:], A_KV_HEADS, A_HEAD_DIM).T.astype(BF16)
    gq = _lane_replicated(q_gain * scale)
    gk = jnp.tile(k_gain, 2)[None, :]
    group = A_HEADS // A_KV_HEADS
    sink_rows = jnp.repeat(sink.reshape(A_KV_HEADS, group), BLOCK, axis=1)[:, None, :]
    return (w_qt, w_k, w_vt, gq, gk), sink_rows


def _prep_b(w_in, w_uq, w_ukv, q_gain, k_gain, q_lat_gain, kv_lat_gain, g):
    scale = (B_QK_DIM ** -0.5) * LOG2E
    half = B_ROPE_DIM // 2
    lat_cols = B_Q_RANK + B_KV_RANK
    rope_cols = w_in[:, lat_cols:]
    rope_sw_cols = jnp.concatenate([rope_cols[:, half:], rope_cols[:, :half]], axis=1)
    slot = lambda cols: jnp.pad(cols, ((0, 0), (B_NOPE_DIM, LANES - B_QK_DIM)))
    w_in_p = jnp.concatenate([w_in[:, :lat_cols], slot(rope_cols), slot(rope_sw_cols)],
                             axis=1).astype(BF16)
    w_uqt = _head_slots(w_uq, B_HEADS, B_QK_DIM).T.astype(BF16)
    ukv = w_ukv.reshape(B_KV_RANK, B_HEADS, B_NOPE_DIM + B_V_DIM)
    w_uk = _head_slots(ukv[:, :, :B_NOPE_DIM].reshape(B_KV_RANK, -1), B_HEADS, B_NOPE_DIM).astype(BF16)
    w_uv = jnp.pad(ukv[:, :, B_NOPE_DIM:], ((0, 0), (0, 0), (0, MLA_VT_ROWS - B_V_DIM)))
    w_uvt = w_uv.reshape(B_KV_RANK, -1).T.astype(BF16)
    pad = (0, LANES - B_QK_DIM)
    gq = _lane_replicated(jnp.pad(q_gain * scale, pad))
    gk = jnp.pad(k_gain, pad)[None, :]
    k_gain_sw = jnp.concatenate([k_gain[:B_NOPE_DIM], k_gain[B_NOPE_DIM + half:],
                                 k_gain[B_NOPE_DIM:B_NOPE_DIM + half]])
    gk_sw = jnp.pad(k_gain_sw, pad)[None, :]
    return [g, w_in_p, q_lat_gain[None, :], kv_lat_gain[None, :], w_uqt, w_uk, w_uvt,
            gq, gk, gk_sw]


def _trunk(x, p, mix_norm, layers_a, layers_b, a_w_o, b_w_o, ffn_norm, w_gu, w_d,
           ple_norm, w_pg, w_pp):
    B, S, _ = x.shape
    depth = p.shape[0]
    T = B * S
    p = p.reshape(depth, T, PLE_DIM)
    rows_a = _row_tables(S, A_ROT_DIM)
    lanes_a = _lane_tables(S, A_ROT_DIM, (0, HALF))
    rows_b = _row_tables(S, B_ROPE_DIM)
    c_b, lo_b, hi_b = _lane_tables(S, B_ROPE_DIM, (B_NOPE_DIM,))
    lanes_b = (c_b, lo_b + hi_b)
    for i in range(depth):
        j = i // 2
        g = mix_norm[i][None, :]
        if i % 2 == 0:
            consts, sink_rows = layers_a[j]
            qt, k, vt = _proj_a(x, g, *consts, rows_a, lanes_a)
            o = _attn_a(sink_rows, qt, k, vt)
            w_o = a_w_o[j]
        else:
            qt, k, vt = _proj_b(x, layers_b[j](g), rows_b, lanes_b)
            o = _attn_b(qt, k, vt)
            w_o = b_w_o[j]
        x = _post(x.reshape(T, D_MODEL), o.reshape(T, -1), p, i, w_o, ffn_norm[i][None, :],
                  w_gu[i], w_d[i], ple_norm[i][None, :], w_pg[i], w_pp[i]).reshape(B, S, D_MODEL)
    return x


def kernel(x_prompt, x_sample, p_prompt, p_sample, mix_norm, a_w_qkv, a_q_norm, a_k_norm, a_sink, a_w_o, b_w_in, b_q_lat_norm, b_kv_lat_norm, b_w_uq, b_w_ukv, b_q_norm, b_k_norm, b_w_o, ffn_norm, ffn_w_gate_up, ffn_w_down, ple_norm, ple_w_gate, ple_w_proj):
    layers_a = [_prep_a(a_w_qkv[j], a_q_norm[j], a_k_norm[j], a_sink[j])
                for j in range(a_w_qkv.shape[0])]
    layers_b = [functools.partial(_prep_b, b_w_in[j], b_w_uq[j], b_w_ukv[j], b_q_norm[j],
                                  b_k_norm[j], b_q_lat_norm[j], b_kv_lat_norm[j])
                for j in range(b_w_in.shape[0])]
    shared = (mix_norm, layers_a, layers_b, a_w_o.astype(BF16), b_w_o.astype(BF16),
              ffn_norm, ffn_w_gate_up.astype(BF16), ffn_w_down.astype(BF16), ple_norm,
              ple_w_gate.astype(BF16), ple_w_proj.astype(BF16))
    return (_trunk(x_prompt, p_prompt, *shared), _trunk(x_sample, p_sample, *shared))
```

```python
import functools
import math

import jax
import jax.numpy as jnp
from jax import lax
from jax.experimental import pallas as pl
from jax.experimental.pallas import tpu as pltpu

F32 = jnp.float32
BF16 = jnp.bfloat16

D_MODEL = 1024
PLE_DIM = 256
RMS_EPS = 1e-6
ROPE_THETA = 500000.0
BLOCK = 128
A_HEADS = 16
A_KV_HEADS = 4
A_HEAD_DIM = 64
A_ROT_DIM = 16
B_HEADS = 16
B_Q_RANK = 384
B_KV_RANK = 128
B_NOPE_DIM = 64
B_ROPE_DIM = 32
B_V_DIM = 64
B_QK_DIM = B_NOPE_DIM + B_ROPE_DIM
FFN_HIDDEN = 2816

LANES = 128
HALF = LANES // 2
VMEM_LIMIT_BYTES = 56 * 1024 * 1024

LOG2E = math.log2(math.e)
MASKED = -1e30
NT_DIMS = (((1,), (1,)), ((), ()))

PROJ_TILE = 512
POST_TILE = 512
MLA_Q_TILE = 512
MLA_KV_TILE = 512
MLA_WINDOW_CHUNKS = 4
MLA_VT_ROWS = 80
ATTN_A_BLOCKS_PER_STEP = 8


def _compiler_params(semantics):
    return pltpu.CompilerParams(dimension_semantics=semantics,
                                vmem_limit_bytes=VMEM_LIMIT_BYTES)


def _resident(shape):
    nd = len(shape)
    return pl.BlockSpec(shape, lambda *_: (0,) * nd, pipeline_mode=pl.Buffered(1))


def _rms(x, g):
    ms = jnp.mean(x * x, axis=-1, keepdims=True)
    return x * lax.rsqrt(ms + RMS_EPS) * g


def _sigmoid(x):
    return 1.0 / (1.0 + jnp.exp(-x))


def _lane_iota(shape):
    return lax.broadcasted_iota(jnp.int32, shape, len(shape) - 1)


def _across_lanes(col_block, n):
    return jnp.concatenate([col_block] * (n // LANES), axis=1)


def _rope_rows(x1, x2, cos_t, sin_t):
    return x1 * cos_t - x2 * sin_t, x1 * sin_t + x2 * cos_t


def _rope_lanes(x, c, s_from_below, s_from_above, shift):
    return (x * c + pltpu.roll(x, shift, 1) * s_from_below
            + pltpu.roll(x, LANES - shift, 1) * s_from_above)


def _angles(seq, rot_dim):
    inv = 1.0 / (ROPE_THETA ** (jnp.arange(0, rot_dim, 2, dtype=F32) / rot_dim))
    ang = jnp.arange(seq, dtype=F32)[:, None] * inv[None, :]
    return jnp.cos(ang), jnp.sin(ang)


def _lane_tables(seq, rot_dim, lane_offsets):
    half = rot_dim // 2
    cos, sin = _angles(seq, rot_dim)

    def place(block, off):
        return jnp.pad(block, ((0, 0), (off, LANES - off - half)))

    c = jnp.zeros((seq, LANES), F32)
    rotated = jnp.zeros((seq, LANES), F32)
    lo = jnp.zeros((seq, LANES), F32)
    hi = jnp.zeros((seq, LANES), F32)
    for off in lane_offsets:
        c = c + place(cos, off) + place(cos, off + half)
        rotated = rotated + place(jnp.ones_like(cos), off) + place(jnp.ones_like(cos), off + half)
        hi = hi + place(-sin, off)
        lo = lo + place(sin, off + half)
    return c + (1.0 - rotated), lo, hi


def _row_tables(seq, rot_dim):
    cos, sin = _angles(seq, rot_dim)
    return cos.T, sin.T


def _ones_row(shape):
    return (lax.broadcasted_iota(jnp.int32, shape, 0) == B_V_DIM).astype(F32)


def _proj_a_kernel(x_ref, g_ref, wqt_ref, wk_ref, wvt_ref, gq_ref, gk_ref, ct_ref, st_ref,
                   c_ref, sl_ref, sh_ref, q_ref, k_ref, vt_ref):
    x = x_ref[0]
    ts = x.shape[0]
    h = _rms(x, g_ref[...]).astype(BF16)
    q_t = lax.dot_general(wqt_ref[...], h, NT_DIMS, preferred_element_type=F32)
    kk = jnp.dot(h, wk_ref[...], preferred_element_type=F32)
    vt = lax.dot_general(wvt_ref[...], h, NT_DIMS, preferred_element_type=F32)

    gq = _across_lanes(gq_ref[...], ts)
    cos_t, sin_t = ct_ref[...], st_ref[...]
    half = A_ROT_DIM // 2
    pad = jnp.zeros((LANES - A_HEAD_DIM, ts), BF16)
    for hd in range(A_HEADS):
        t = q_t[hd * A_HEAD_DIM:(hd + 1) * A_HEAD_DIM, :]
        ms = jnp.sum(t * t, axis=0, keepdims=True) / A_HEAD_DIM
        tn = t * lax.rsqrt(ms + RMS_EPS) * gq
        o1, o2 = _rope_rows(tn[:half], tn[half:A_ROT_DIM], cos_t, sin_t)
        out = jnp.concatenate([o1, o2, tn[A_ROT_DIM:]], axis=0)
        q_ref[0, hd, :A_HEAD_DIM, :] = out.astype(BF16)
        q_ref[0, hd, A_HEAD_DIM:, :] = pad

    c, sl, sh = c_ref[...], sl_ref[...], sh_ref[...]
    is_lo = _lane_iota((ts, LANES)) < HALF
    for j in range(A_KV_HEADS * A_HEAD_DIM // LANES):
        t = kk[:, j * LANES:(j + 1) * LANES]
        sq = t * t
        ss_lo = jnp.sum(jnp.where(is_lo, sq, 0.0), axis=-1, keepdims=True)
        ss_hi = jnp.sum(jnp.where(is_lo, 0.0, sq), axis=-1, keepdims=True)
        r = jnp.where(is_lo, lax.rsqrt(ss_lo / A_HEAD_DIM + RMS_EPS),
                      lax.rsqrt(ss_hi / A_HEAD_DIM + RMS_EPS))
        t = _rope_lanes(t * r * gk_ref[...], c, sl, sh, half)
        k_ref[0, 2 * j] = jnp.where(is_lo, t, 0.0).astype(BF16)
        k_ref[0, 2 * j + 1] = jnp.where(is_lo, pltpu.roll(t, HALF, 1), 0.0).astype(BF16)

    ones = _ones_row((LANES, ts))
    for g in range(A_KV_HEADS):
        blk = (vt[g * LANES:(g + 1) * LANES, :] + ones).astype(BF16)
        for cc in range(ts // BLOCK):
            vt_ref[0, g, cc] = blk[:, cc * BLOCK:(cc + 1) * BLOCK]


def _proj_a(x, g, w_qt, w_k, w_vt, gq, gk, row_tabs, lane_tabs):
    B, S, _ = x.shape
    ts = min(PROJ_TILE, S)
    rtab = pl.BlockSpec((A_ROT_DIM // 2, ts), lambda b, s: (0, s))
    ltab = pl.BlockSpec((ts, LANES), lambda b, s: (s, 0))
    consts = [g, w_qt, w_k, w_vt, gq, gk]
    return pl.pallas_call(
        _proj_a_kernel,
        grid=(B, S // ts),
        in_specs=[pl.BlockSpec((1, ts, D_MODEL), lambda b, s: (b, s, 0))]
                 + [_resident(a.shape) for a in consts] + [rtab, rtab, ltab, ltab, ltab],
        out_specs=[pl.BlockSpec((1, A_HEADS, LANES, ts), lambda b, s: (b, 0, 0, s)),
                   pl.BlockSpec((1, A_KV_HEADS, ts, LANES), lambda b, s: (b, 0, s, 0)),
                   pl.BlockSpec((1, A_KV_HEADS, ts // BLOCK, LANES, BLOCK),
                                lambda b, s: (b, 0, s, 0, 0))],
        out_shape=[jax.ShapeDtypeStruct((B, A_HEADS, LANES, S), BF16),
                   jax.ShapeDtypeStruct((B, A_KV_HEADS, S, LANES), BF16),
                   jax.ShapeDtypeStruct((B, A_KV_HEADS, S // BLOCK, LANES, BLOCK), BF16)],
        compiler_params=_compiler_params(("parallel", "parallel")),
        name="proj_a",
    )(x, *consts, *row_tabs, *lane_tabs)


def _attn_a_kernel(sink_ref, qt_ref, k_ref, vt_ref, o_ref, *, blocks_per_step):
    step = pl.program_id(2)
    nb = vt_ref.shape[2]
    group = A_HEADS // A_KV_HEADS
    width = group * BLOCK
    key = lax.broadcasted_iota(jnp.int32, (BLOCK, width), 0)
    qry = lax.broadcasted_iota(jnp.int32, (BLOCK, width), 1) & (BLOCK - 1)
    sink = sink_ref[0] * LOG2E
    for r in range(blocks_per_step):
        n = step * blocks_per_step + r
        rows = slice(r * BLOCK, (r + 1) * BLOCK)
        q4t = jnp.concatenate([qt_ref[0, hd, :, rows] for hd in range(group)], axis=1)
        i_prev = jnp.maximum(n - 1, 0)
        i_next = jnp.minimum(n + 1, nb - 1)

        def scores(i, q4t=q4t):
            k = k_ref[0, 0, pl.ds(pl.multiple_of(i * BLOCK, BLOCK), BLOCK), :]
            return jnp.dot(k, q4t, preferred_element_type=F32)

        s_p = jnp.where((key >= qry) & (n > 0), scores(i_prev), MASKED)
        s_c = scores(n)
        s_n = jnp.where((key <= qry) & (n < nb - 1), scores(i_next), MASKED)
        m = jnp.max(jnp.maximum(jnp.maximum(s_p, s_c), s_n), axis=0, keepdims=True)
        m = jnp.maximum(m, sink)
        acc = jnp.dot(vt_ref[0, 0, i_prev], jnp.exp2(s_p - m).astype(BF16),
                      preferred_element_type=F32)
        acc += jnp.dot(vt_ref[0, 0, n], jnp.exp2(s_c - m).astype(BF16),
                       preferred_element_type=F32)
        acc += jnp.dot(vt_ref[0, 0, i_next], jnp.exp2(s_n - m).astype(BF16),
                       preferred_element_type=F32)
        denom = acc[B_V_DIM:B_V_DIM + 1] + jnp.exp2(sink - m)
        o = acc[:B_V_DIM] / denom
        for pi in range(group // 2):
            pair = jnp.concatenate([o[:, (2 * pi) * BLOCK:(2 * pi + 1) * BLOCK],
                                    o[:, (2 * pi + 1) * BLOCK:(2 * pi + 2) * BLOCK]], axis=0)
            o_ref[0, rows, pi * LANES:(pi + 1) * LANES] = pair.T.astype(BF16)


def _attn_a(sink_rows, qt, k, vt):
    B, _, _, S = qt.shape
    nb = S // BLOCK
    r = min(ATTN_A_BLOCKS_PER_STEP, nb)
    group = A_HEADS // A_KV_HEADS
    return pl.pallas_call(
        functools.partial(_attn_a_kernel, blocks_per_step=r),
        grid=(B, A_KV_HEADS, nb // r),
        in_specs=[pl.BlockSpec((1, 1, group * BLOCK), lambda b, g, n: (g, 0, 0)),
                  pl.BlockSpec((1, group, LANES, r * BLOCK), lambda b, g, n: (b, g, 0, n)),
                  pl.BlockSpec((1, 1, S, LANES), lambda b, g, n: (b, g, 0, 0)),
                  pl.BlockSpec((1, 1, nb, LANES, BLOCK), lambda b, g, n: (b, g, 0, 0, 0))],
        out_specs=pl.BlockSpec((1, r * BLOCK, group * A_HEAD_DIM), lambda b, g, n: (b, n, g)),
        out_shape=jax.ShapeDtypeStruct((B, S, A_HEADS * A_HEAD_DIM), BF16),
        compiler_params=_compiler_params(("parallel", "parallel", "arbitrary")),
        name="attn_a",
    )(sink_rows, qt, k, vt)


def _proj_b_kernel(x_ref, g_ref, win_ref, gql_ref, gkvl_ref, wuqt_ref, wuk_ref, wuvt_ref,
                   gq_ref, gk_ref, gksw_ref, ct_ref, st_ref, c_ref, ssg_ref,
                   q_ref, k_ref, vt_ref):
    x = x_ref[0]
    ts = x.shape[0]
    h = _rms(x, g_ref[...]).astype(BF16)
    lat = jnp.dot(h, win_ref[...], preferred_element_type=F32)
    cq = _rms(lat[:, :B_Q_RANK], gql_ref[...]).astype(BF16)
    ckv = _rms(lat[:, B_Q_RANK:B_Q_RANK + B_KV_RANK], gkvl_ref[...]).astype(BF16)
    rope_off = B_Q_RANK + B_KV_RANK
    k_rope = lat[:, rope_off:rope_off + LANES]
    k_rope_sw = lat[:, rope_off + LANES:]
    q_t = lax.dot_general(wuqt_ref[...], cq, NT_DIMS, preferred_element_type=F32)
    kn_all = jnp.dot(ckv, wuk_ref[...], preferred_element_type=F32)
    vt_all = lax.dot_general(wuvt_ref[...], ckv, NT_DIMS, preferred_element_type=F32)

    gq = _across_lanes(gq_ref[...], ts)
    cos_t, sin_t = ct_ref[...], st_ref[...]
    half = B_ROPE_DIM // 2
    gain_cos = c_ref[...] * gk_ref[...]
    rope_term = k_rope_sw * (ssg_ref[...] * gksw_ref[...])
    ones = _ones_row((MLA_VT_ROWS, ts))
    tk = vt_ref.shape[4]
    for hd in range(B_HEADS):
        sl_h = slice(hd * LANES, (hd + 1) * LANES)
        t = q_t[sl_h, :]
        ms = jnp.sum(t * t, axis=0, keepdims=True) / B_QK_DIM
        tn = t * lax.rsqrt(ms + RMS_EPS) * gq
        o1, o2 = _rope_rows(tn[B_NOPE_DIM:B_NOPE_DIM + half], tn[B_NOPE_DIM + half:B_QK_DIM],
                            cos_t, sin_t)
        q_ref[0, hd] = jnp.concatenate([tn[:B_NOPE_DIM], o1, o2, tn[B_QK_DIM:]],
                                       axis=0).astype(BF16)
        xk = kn_all[:, sl_h] + k_rope
        msk = jnp.sum(xk * xk, axis=-1, keepdims=True) / B_QK_DIM
        k_ref[0, hd] = ((xk * gain_cos + rope_term) * lax.rsqrt(msk + RMS_EPS)).astype(BF16)
        blk = (vt_all[hd * MLA_VT_ROWS:(hd + 1) * MLA_VT_ROWS, :] + ones).astype(BF16)
        for cc in range(ts // tk):
            vt_ref[0, hd, cc] = blk[:, cc * tk:(cc + 1) * tk]


def _proj_b(x, consts, row_tabs, lane_tabs):
    B, S, _ = x.shape
    ts = min(PROJ_TILE, S)
    tk = min(MLA_KV_TILE, max(LANES, S // 8))
    rtab = pl.BlockSpec((B_ROPE_DIM // 2, ts), lambda b, s: (0, s))
    ltab = pl.BlockSpec((ts, LANES), lambda b, s: (s, 0))
    return pl.pallas_call(
        _proj_b_kernel,
        grid=(B, S // ts),
        in_specs=[pl.BlockSpec((1, ts, D_MODEL), lambda b, s: (b, s, 0))]
                 + [_resident(a.shape) for a in consts] + [rtab, rtab, ltab, ltab],
        out_specs=[pl.BlockSpec((1, B_HEADS, LANES, ts), lambda b, s: (b, 0, 0, s)),
                   pl.BlockSpec((1, B_HEADS, ts, LANES), lambda b, s: (b, 0, s, 0)),
                   pl.BlockSpec((1, B_HEADS, ts // tk, MLA_VT_ROWS, tk),
                                lambda b, s: (b, 0, s, 0, 0))],
        out_shape=[jax.ShapeDtypeStruct((B, B_HEADS, LANES, S), BF16),
                   jax.ShapeDtypeStruct((B, B_HEADS, S, LANES), BF16),
                   jax.ShapeDtypeStruct((B, B_HEADS, S // tk, MLA_VT_ROWS, tk), BF16)],
        compiler_params=_compiler_params(("parallel", "parallel")),
        name="proj_b",
    )(x, *consts, *row_tabs, *lane_tabs)


def _attn_b_kernel(qt_ref, k_ref, vt_ref, o_ref, s0_ref, s1_ref, m_ref, acc_ref, *,
                   window_chunks):
    n_chunks, tk = vt_ref.shape[2], vt_ref.shape[4]
    tq = qt_ref.shape[3]
    m_ref[...] = jnp.full(m_ref.shape, MASKED, F32)
    acc_ref[...] = jnp.zeros(acc_ref.shape, F32)

    def scores(c, s_ref):
        start = pl.multiple_of(c * tk, tk)
        col_max = []
        for hh in range(2):
            s = jnp.dot(k_ref[0, hh, pl.ds(start, tk), :], qt_ref[0, hh],
                        preferred_element_type=F32)
            s_ref[hh, :, :tq] = s
            col_max.append(jnp.max(s, axis=0, keepdims=True))
        return tuple(col_max)

    def accumulate(c, s_ref, col_max):
        for hh in range(2):
            m_old = m_ref[hh]
            m_new = jnp.maximum(m_old, col_max[hh])
            p = jnp.exp2(s_ref[hh, :, :tq] - m_new).astype(BF16)
            acc_ref[hh] = (acc_ref[hh] * jnp.exp2(m_old - m_new)
                           + jnp.dot(vt_ref[0, hh, c], p, preferred_element_type=F32))
            m_ref[hh] = m_new

    bufs = (s0_ref, s1_ref)

    def window(c0, col_max, last):
        for w in range(window_chunks):
            is_final = last and w == window_chunks - 1
            nxt = None if is_final else scores(c0 + w + 1, bufs[(w + 1) % 2])
            accumulate(c0 + w, bufs[w % 2], col_max)
            col_max = nxt
        return col_max

    n_windows = n_chunks // window_chunks
    col_max = lax.fori_loop(0, n_windows - 1,
                            lambda i, mx: window(i * window_chunks, mx, False),
                            scores(0, s0_ref))
    window((n_windows - 1) * window_chunks, col_max, True)
    halves = [acc_ref[hh][:B_V_DIM] / acc_ref[hh][B_V_DIM:B_V_DIM + 1] for hh in range(2)]
    o_ref[0] = jnp.concatenate(halves, axis=0).T.astype(BF16)


def _attn_b(qt, k, vt):
    B, H, _, S = qt.shape
    tq = min(MLA_Q_TILE, S)
    n_chunks, tk = vt.shape[2], vt.shape[4]
    window = min(MLA_WINDOW_CHUNKS, n_chunks)
    assert n_chunks % window == 0 and window % 2 == 0
    score_buf = pltpu.VMEM((2, tk, tq + LANES), F32)
    return pl.pallas_call(
        functools.partial(_attn_b_kernel, window_chunks=window),
        grid=(B, H // 2, S // tq),
        in_specs=[pl.BlockSpec((1, 2, LANES, tq), lambda b, j, i: (b, j, 0, i)),
                  pl.BlockSpec((1, 2, S, LANES), lambda b, j, i: (b, j, 0, 0)),
                  pl.BlockSpec((1, 2, n_chunks, MLA_VT_ROWS, tk), lambda b, j, i: (b, j, 0, 0, 0))],
        out_specs=pl.BlockSpec((1, tq, LANES), lambda b, j, i: (b, i, j)),
        out_shape=jax.ShapeDtypeStruct((B, S, H * B_V_DIM), BF16),
        scratch_shapes=[score_buf, score_buf, pltpu.VMEM((2, 1, tq), F32),
                        pltpu.VMEM((2, MLA_VT_ROWS, tq), F32)],
        compiler_params=_compiler_params(("parallel", "parallel", "arbitrary")),
        name="attn_b",
    )(qt, k, vt)


def _post_kernel(x_ref, o_ref, p_ref, wo_ref, fg_ref, wgu_ref, wd_ref, pg_ref, wpg_ref,
                 wpp_ref, out_ref):
    x = x_ref[...] + jnp.dot(o_ref[...], wo_ref[...], preferred_element_type=F32)
    h = _rms(x, fg_ref[...]).astype(BF16)
    gu = jnp.dot(h, wgu_ref[...], preferred_element_type=F32)
    gate, up = gu[:, :FFN_HIDDEN], gu[:, FFN_HIDDEN:]
    act = (gate * _sigmoid(gate) * up).astype(BF16)
    x = x + jnp.dot(act, wd_ref[...], preferred_element_type=F32)
    h = _rms(x, pg_ref[...]).astype(BF16)
    ple_gate = _sigmoid(jnp.dot(h, wpg_ref[...], preferred_element_type=F32))
    emb = jnp.dot(p_ref[0].astype(BF16), wpp_ref[...], preferred_element_type=F32)
    out_ref[...] = x + ple_gate * emb


def _post(x, o, p, layer, wo, fg, wgu, wd, pg, wpg, wpp):
    T = x.shape[0]
    tm = min(POST_TILE, T)
    consts = [wo, fg, wgu, wd, pg, wpg, wpp]
    row = lambda width: pl.BlockSpec((tm, width), lambda t: (t, 0))
    return pl.pallas_call(
        _post_kernel,
        grid=(T // tm,),
        in_specs=[row(D_MODEL), row(o.shape[1]),
                  pl.BlockSpec((1, tm, PLE_DIM), lambda t: (layer, t, 0))]
                 + [_resident(a.shape) for a in consts],
        out_specs=row(D_MODEL),
        out_shape=jax.ShapeDtypeStruct((T, D_MODEL), F32),
        compiler_params=_compiler_params(("parallel",)),
        name="post",
    )(x, o, p, *consts)


def _head_slots(w, heads, width):
    kdim = w.shape[0]
    w = w.reshape(kdim, heads, width)
    return jnp.pad(w, ((0, 0), (0, 0), (0, LANES - width))).reshape(kdim, heads * LANES)


def _lane_replicated(v):
    return jnp.broadcast_to(v[:, None], (v.shape[0], LANES))


def _prep_a(w_qkv, q_gain, k_gain, sink):
    scale = (A_HEAD_DIM ** -0.5) * LOG2E
    n_q = A_HEADS * A_HEAD_DIM
    n_qk = n_q + A_KV_HEADS * A_HEAD_DIM
    w_qt = w_qkv[:, :n_q].T.astype(BF16)
    w_k = w_qkv[:, n_q:n_qk].astype(BF16)
    w_vt = _head_slots(w_qkv[:, n_qk:], A_KV_HEADS, A_HEAD_DIM).T.astype(BF16)
    gq = _lane_replicated(q_gain * scale)
    gk = jnp.tile(k_gain, 2)[None, :]
    group = A_HEADS // A_KV_HEADS
    sink_rows = jnp.repeat(sink.reshape(A_KV_HEADS, group), BLOCK, axis=1)[:, None, :]
    return (w_qt, w_k, w_vt, gq, gk), sink_rows


def _prep_b(w_in, w_uq, w_ukv, q_gain, k_gain, q_lat_gain, kv_lat_gain, g):
    scale = (B_QK_DIM ** -0.5) * LOG2E
    half = B_ROPE_DIM // 2
    lat_cols = B_Q_RANK + B_KV_RANK
    rope_cols = w_in[:, lat_cols:]
    rope_sw_cols = jnp.concatenate([rope_cols[:, half:], rope_cols[:, :half]], axis=1)
    slot = lambda cols: jnp.pad(cols, ((0, 0), (B_NOPE_DIM, LANES - B_QK_DIM)))
    w_in_p = jnp.concatenate([w_in[:, :lat_cols], slot(rope_cols), slot(rope_sw_cols)],
                             axis=1).astype(BF16)
    w_uqt = _head_slots(w_uq, B_HEADS, B_QK_DIM).T.astype(BF16)
    ukv = w_ukv.reshape(B_KV_RANK, B_HEADS, B_NOPE_DIM + B_V_DIM)
    w_uk = _head_slots(ukv[:, :, :B_NOPE_DIM].reshape(B_KV_RANK, -1), B_HEADS, B_NOPE_DIM).astype(BF16)
    w_uv = jnp.pad(ukv[:, :, B_NOPE_DIM:], ((0, 0), (0, 0), (0, MLA_VT_ROWS - B_V_DIM)))
    w_uvt = w_uv.reshape(B_KV_RANK, -1).T.astype(BF16)
    pad = (0, LANES - B_QK_DIM)
    gq = _lane_replicated(jnp.pad(q_gain * scale, pad))
    gk = jnp.pad(k_gain, pad)[None, :]
    k_gain_sw = jnp.concatenate([k_gain[:B_NOPE_DIM], k_gain[B_NOPE_DIM + half:],
                                 k_gain[B_NOPE_DIM:B_NOPE_DIM + half]])
    gk_sw = jnp.pad(k_gain_sw, pad)[None, :]
    return [g, w_in_p, q_lat_gain[None, :], kv_lat_gain[None, :], w_uqt, w_uk, w_uvt,
            gq, gk, gk_sw]


def _trunk(x, p, mix_norm, layers_a, layers_b, a_w_o, b_w_o, ffn_norm, w_gu, w_d,
           ple_norm, w_pg, w_pp):
    B, S, _ = x.shape
    depth = p.shape[0]
    T = B * S
    p = p.reshape(depth, T, PLE_DIM)
    rows_a = _row_tables(S, A_ROT_DIM)
    lanes_a = _lane_tables(S, A_ROT_DIM, (0, HALF))
    rows_b = _row_tables(S, B_ROPE_DIM)
    c_b, lo_b, hi_b = _lane_tables(S, B_ROPE_DIM, (B_NOPE_DIM,))
    lanes_b = (c_b, lo_b + hi_b)
    for i in range(depth):
        j = i // 2
        g = mix_norm[i][None, :]
        if i % 2 == 0:
            consts, sink_rows = layers_a[j]
            qt, k, vt = _proj_a(x, g, *consts, rows_a, lanes_a)
            o = _attn_a(sink_rows, qt, k, vt)
            w_o = a_w_o[j]
        else:
            qt, k, vt = _proj_b(x, layers_b[j](g), rows_b, lanes_b)
            o = _attn_b(qt, k, vt)
            w_o = b_w_o[j]
        x = _post(x.reshape(T, D_MODEL), o.reshape(T, -1), p, i, w_o, ffn_norm[i][None, :],
                  w_gu[i], w_d[i], ple_norm[i][None, :], w_pg[i], w_pp[i]).reshape(B, S, D_MODEL)
    return x


def kernel(x_prompt, x_sample, p_prompt, p_sample, mix_norm, a_w_qkv, a_q_norm, a_k_norm, a_sink, a_w_o, b_w_in, b_q_lat_norm, b_kv_lat_norm, b_w_uq, b_w_ukv, b_q_norm, b_k_norm, b_w_o, ffn_norm, ffn_w_gate_up, ffn_w_down, ple_norm, ple_w_gate, ple_w_proj):
    layers_a = [_prep_a(a_w_qkv[j], a_q_norm[j], a_k_norm[j], a_sink[j])
                for j in range(a_w_qkv.shape[0])]
    layers_b = [functools.partial(_prep_b, b_w_in[j], b_w_uq[j], b_w_ukv[j], b_q_norm[j],
                                  b_k_norm[j], b_q_lat_norm[j], b_kv_lat_norm[j])
                for j in range(b_w_in.shape[0])]
    shared = (mix_norm, layers_a, layers_b, a_w_o.astype(BF16), b_w_o.astype(BF16),
              ffn_norm, ffn_w_gate_up.astype(BF16), ffn_w_down.astype(BF16), ple_norm,
              ple_w_gate.astype(BF16), ple_w_proj.astype(BF16))
    return (_trunk(x_prompt, p_prompt, *shared), _trunk(x_sample, p_sample, *shared))
```

```python
import functools
import math

import jax
import jax.numpy as jnp
from jax import lax
from jax.experimental import pallas as pl
from jax.experimental.pallas import tpu as pltpu

F32 = jnp.float32
BF16 = jnp.bfloat16

D_MODEL = 1024
PLE_DIM = 256
RMS_EPS = 1e-6
ROPE_THETA = 500000.0
BLOCK = 128
A_HEADS = 16
A_KV_HEADS = 4
A_HEAD_DIM = 64
A_ROT_DIM = 16
B_HEADS = 16
B_Q_RANK = 384
B_KV_RANK = 128
B_NOPE_DIM = 64
B_ROPE_DIM = 32
B_V_DIM = 64
B_QK_DIM = B_NOPE_DIM + B_ROPE_DIM
FFN_HIDDEN = 2816

LANES = 128
HALF = LANES // 2
VMEM_LIMIT_BYTES = 56 * 1024 * 1024

LOG2E = math.log2(math.e)
MASKED = -1e30
NT_DIMS = (((1,), (1,)), ((), ()))

PROJ_TILE = 512
POST_TILE = 512
MLA_Q_TILE = 512
MLA_KV_TILES = (512, 256)
MLA_WINDOW_CHUNKS = 4
MLA_VT_ROWS = 80
ATTN_A_BLOCKS_PER_STEP = 8


def _compiler_params(semantics):
    return pltpu.CompilerParams(dimension_semantics=semantics,
                                vmem_limit_bytes=VMEM_LIMIT_BYTES)


def _resident(shape):
    nd = len(shape)
    return pl.BlockSpec(shape, lambda *_: (0,) * nd, pipeline_mode=pl.Buffered(1))


def _rms(x, g):
    ms = jnp.mean(x * x, axis=-1, keepdims=True)
    return x * lax.rsqrt(ms + RMS_EPS) * g


def _sigmoid(x):
    return 1.0 / (1.0 + jnp.exp(-x))


def _lane_iota(shape):
    return lax.broadcasted_iota(jnp.int32, shape, len(shape) - 1)


def _across_lanes(col_block, n):
    return jnp.concatenate([col_block] * (n // LANES), axis=1)


def _rope_rows(x1, x2, cos_t, sin_t):
    return x1 * cos_t - x2 * sin_t, x1 * sin_t + x2 * cos_t


def _rope_lanes(x, c, s_from_below, s_from_above, shift):
    return (x * c + pltpu.roll(x, shift, 1) * s_from_below
            + pltpu.roll(x, LANES - shift, 1) * s_from_above)


def _angles(seq, rot_dim):
    inv = 1.0 / (ROPE_THETA ** (jnp.arange(0, rot_dim, 2, dtype=F32) / rot_dim))
    ang = jnp.arange(seq, dtype=F32)[:, None] * inv[None, :]
    return jnp.cos(ang), jnp.sin(ang)


def _lane_tables(seq, rot_dim, lane_offsets):
    half = rot_dim // 2
    cos, sin = _angles(seq, rot_dim)

    def place(block, off):
        return jnp.pad(block, ((0, 0), (off, LANES - off - half)))

    c = jnp.zeros((seq, LANES), F32)
    rotated = jnp.zeros((seq, LANES), F32)
    lo = jnp.zeros((seq, LANES), F32)
    hi = jnp.zeros((seq, LANES), F32)
    for off in lane_offsets:
        c = c + place(cos, off) + place(cos, off + half)
        rotated = rotated + place(jnp.ones_like(cos), off) + place(jnp.ones_like(cos), off + half)
        hi = hi + place(-sin, off)
        lo = lo + place(sin, off + half)
    return c + (1.0 - rotated), lo, hi


def _row_tables(seq, rot_dim):
    cos, sin = _angles(seq, rot_dim)
    return cos.T, sin.T


def _ones_row(shape):
    return (lax.broadcasted_iota(jnp.int32, shape, 0) == B_V_DIM).astype(F32)


def _proj_a_kernel(x_ref, g_ref, wqt_ref, wk_ref, wvt_ref, gq_ref, gk_ref, ct_ref, st_ref,
                   c_ref, sl_ref, sh_ref, q_ref, k_ref, vt_ref):
    x = x_ref[0]
    ts = x.shape[0]
    h = _rms(x, g_ref[...]).astype(BF16)
    q_t = lax.dot_general(wqt_ref[...], h, NT_DIMS, preferred_element_type=F32)
    kk = jnp.dot(h, wk_ref[...], preferred_element_type=F32)
    vt = lax.dot_general(wvt_ref[...], h, NT_DIMS, preferred_element_type=F32)

    gq = _across_lanes(gq_ref[...], ts)
    cos_t, sin_t = ct_ref[...], st_ref[...]
    half = A_ROT_DIM // 2
    pad = jnp.zeros((LANES - A_HEAD_DIM, ts), BF16)
    for hd in range(A_HEADS):
        t = q_t[hd * A_HEAD_DIM:(hd + 1) * A_HEAD_DIM, :]
        ms = jnp.sum(t * t, axis=0, keepdims=True) / A_HEAD_DIM
        tn = t * lax.rsqrt(ms + RMS_EPS) * gq
        o1, o2 = _rope_rows(tn[:half], tn[half:A_ROT_DIM], cos_t, sin_t)
        out = jnp.concatenate([o1, o2, tn[A_ROT_DIM:]], axis=0)
        q_ref[0, hd, :A_HEAD_DIM, :] = out.astype(BF16)
        q_ref[0, hd, A_HEAD_DIM:, :] = pad

    c, sl, sh = c_ref[...], sl_ref[...], sh_ref[...]
    is_lo = _lane_iota((ts, LANES)) < HALF
    for j in range(A_KV_HEADS * A_HEAD_DIM // LANES):
        t = kk[:, j * LANES:(j + 1) * LANES]
        sq = t * t
        ss_lo = jnp.sum(jnp.where(is_lo, sq, 0.0), axis=-1, keepdims=True)
        ss_hi = jnp.sum(jnp.where(is_lo, 0.0, sq), axis=-1, keepdims=True)
        r = jnp.where(is_lo, lax.rsqrt(ss_lo / A_HEAD_DIM + RMS_EPS),
                      lax.rsqrt(ss_hi / A_HEAD_DIM + RMS_EPS))
        t = _rope_lanes(t * r * gk_ref[...], c, sl, sh, half)
        k_ref[0, 2 * j] = jnp.where(is_lo, t, 0.0).astype(BF16)
        k_ref[0, 2 * j + 1] = jnp.where(is_lo, pltpu.roll(t, HALF, 1), 0.0).astype(BF16)

    ones = _ones_row((LANES, ts))
    for g in range(A_KV_HEADS):
        blk = (vt[g * LANES:(g + 1) * LANES, :] + ones).astype(BF16)
        for cc in range(ts // BLOCK):
            vt_ref[0, g, cc] = blk[:, cc * BLOCK:(cc + 1) * BLOCK]


def _proj_a(x, g, w_qt, w_k, w_vt, gq, gk, row_tabs, lane_tabs):
    B, S, _ = x.shape
    ts = min(PROJ_TILE, S)
    rtab = pl.BlockSpec((A_ROT_DIM // 2, ts), lambda b, s: (0, s))
    ltab = pl.BlockSpec((ts, LANES), lambda b, s: (s, 0))
    consts = [g, w_qt, w_k, w_vt, gq, gk]
    return pl.pallas_call(
        _proj_a_kernel,
        grid=(B, S // ts),
        in_specs=[pl.BlockSpec((1, ts, D_MODEL), lambda b, s: (b, s, 0))]
                 + [_resident(a.shape) for a in consts] + [rtab, rtab, ltab, ltab, ltab],
        out_specs=[pl.BlockSpec((1, A_HEADS, LANES, ts), lambda b, s: (b, 0, 0, s)),
                   pl.BlockSpec((1, A_KV_HEADS, ts, LANES), lambda b, s: (b, 0, s, 0)),
                   pl.BlockSpec((1, A_KV_HEADS, ts // BLOCK, LANES, BLOCK),
                                lambda b, s: (b, 0, s, 0, 0))],
        out_shape=[jax.ShapeDtypeStruct((B, A_HEADS, LANES, S), BF16),
                   jax.ShapeDtypeStruct((B, A_KV_HEADS, S, LANES), BF16),
                   jax.ShapeDtypeStruct((B, A_KV_HEADS, S // BLOCK, LANES, BLOCK), BF16)],
        compiler_params=_compiler_params(("parallel", "parallel")),
        name="proj_a",
    )(x, *consts, *row_tabs, *lane_tabs)


def _attn_a_kernel(sink_ref, qt_ref, k_ref, vt_ref, o_ref, *, blocks_per_step):
    step = pl.program_id(2)
    nb = vt_ref.shape[2]
    group = A_HEADS // A_KV_HEADS
    width = group * BLOCK
    key = lax.broadcasted_iota(jnp.int32, (BLOCK, width), 0)
    qry = lax.broadcasted_iota(jnp.int32, (BLOCK, width), 1) & (BLOCK - 1)
    sink = sink_ref[0] * LOG2E
    for r in range(blocks_per_step):
        n = step * blocks_per_step + r
        rows = slice(r * BLOCK, (r + 1) * BLOCK)
        q4t = jnp.concatenate([qt_ref[0, hd, :, rows] for hd in range(group)], axis=1)
        i_prev = jnp.maximum(n - 1, 0)
        i_next = jnp.minimum(n + 1, nb - 1)

        def scores(i, q4t=q4t):
            k = k_ref[0, 0, pl.ds(pl.multiple_of(i * BLOCK, BLOCK), BLOCK), :]
            return jnp.dot(k, q4t, preferred_element_type=F32)

        s_p = jnp.where((key >= qry) & (n > 0), scores(i_prev), MASKED)
        s_c = scores(n)
        s_n = jnp.where((key <= qry) & (n < nb - 1), scores(i_next), MASKED)
        m = jnp.max(jnp.maximum(jnp.maximum(s_p, s_c), s_n), axis=0, keepdims=True)
        m = jnp.maximum(m, sink)
        acc = jnp.dot(vt_ref[0, 0, i_prev], jnp.exp2(s_p - m).astype(BF16),
                      preferred_element_type=F32)
        acc += jnp.dot(vt_ref[0, 0, n], jnp.exp2(s_c - m).astype(BF16),
                       preferred_element_type=F32)
        acc += jnp.dot(vt_ref[0, 0, i_next], jnp.exp2(s_n - m).astype(BF16),
                       preferred_element_type=F32)
        denom = acc[B_V_DIM:B_V_DIM + 1] + jnp.exp2(sink - m)
        o = acc[:B_V_DIM] / denom
        for pi in range(group // 2):
            pair = jnp.concatenate([o[:, (2 * pi) * BLOCK:(2 * pi + 1) * BLOCK],
                                    o[:, (2 * pi + 1) * BLOCK:(2 * pi + 2) * BLOCK]], axis=0)
            o_ref[0, rows, pi * LANES:(pi + 1) * LANES] = pair.T.astype(BF16)


def _attn_a(sink_rows, qt, k, vt):
    B, _, _, S = qt.shape
    nb = S // BLOCK
    r = min(ATTN_A_BLOCKS_PER_STEP, nb)
    group = A_HEADS // A_KV_HEADS
    return pl.pallas_call(
        functools.partial(_attn_a_kernel, blocks_per_step=r),
        grid=(B, A_KV_HEADS, nb // r),
        in_specs=[pl.BlockSpec((1, 1, group * BLOCK), lambda b, g, n: (g, 0, 0)),
                  pl.BlockSpec((1, group, LANES, r * BLOCK), lambda b, g, n: (b, g, 0, n)),
                  pl.BlockSpec((1, 1, S, LANES), lambda b, g, n: (b, g, 0, 0)),
                  pl.BlockSpec((1, 1, nb, LANES, BLOCK), lambda b, g, n: (b, g, 0, 0, 0))],
        out_specs=pl.BlockSpec((1, r * BLOCK, group * A_HEAD_DIM), lambda b, g, n: (b, n, g)),
        out_shape=jax.ShapeDtypeStruct((B, S, A_HEADS * A_HEAD_DIM), BF16),
        compiler_params=_compiler_params(("parallel", "parallel", "arbitrary")),
        name="attn_a",
    )(sink_rows, qt, k, vt)


def _proj_b_kernel(x_ref, g_ref, win_ref, gql_ref, gkvl_ref, wuqt_ref, wuk_ref, wuvt_ref,
                   gq_ref, gk_ref, gksw_ref, ct_ref, st_ref, c_ref, ssg_ref,
                   q_ref, k_ref, vt_ref):
    x = x_ref[0]
    ts = x.shape[0]
    h = _rms(x, g_ref[...]).astype(BF16)
    lat = jnp.dot(h, win_ref[...], preferred_element_type=F32)
    cq = _rms(lat[:, :B_Q_RANK], gql_ref[...]).astype(BF16)
    ckv = _rms(lat[:, B_Q_RANK:B_Q_RANK + B_KV_RANK], gkvl_ref[...]).astype(BF16)
    rope_off = B_Q_RANK + B_KV_RANK
    k_rope = lat[:, rope_off:rope_off + LANES]
    k_rope_sw = lat[:, rope_off + LANES:]
    q_t = lax.dot_general(wuqt_ref[...], cq, NT_DIMS, preferred_element_type=F32)
    kn_all = jnp.dot(ckv, wuk_ref[...], preferred_element_type=F32)
    vt_all = lax.dot_general(wuvt_ref[...], ckv, NT_DIMS, preferred_element_type=F32)

    gq = _across_lanes(gq_ref[...], ts)
    cos_t, sin_t = ct_ref[...], st_ref[...]
    half = B_ROPE_DIM // 2
    gain_cos = c_ref[...] * gk_ref[...]
    rope_term = k_rope_sw * (ssg_ref[...] * gksw_ref[...])
    ones = _ones_row((MLA_VT_ROWS, ts))
    tk = vt_ref.shape[4]
    for hd in range(B_HEADS):
        sl_h = slice(hd * LANES, (hd + 1) * LANES)
        t = q_t[sl_h, :]
        ms = jnp.sum(t * t, axis=0, keepdims=True) / B_QK_DIM
        tn = t * lax.rsqrt(ms + RMS_EPS) * gq
        o1, o2 = _rope_rows(tn[B_NOPE_DIM:B_NOPE_DIM + half], tn[B_NOPE_DIM + half:B_QK_DIM],
                            cos_t, sin_t)
        q_ref[0, hd] = jnp.concatenate([tn[:B_NOPE_DIM], o1, o2, tn[B_QK_DIM:]],
                                       axis=0).astype(BF16)
        xk = kn_all[:, sl_h] + k_rope
        msk = jnp.sum(xk * xk, axis=-1, keepdims=True) / B_QK_DIM
        k_ref[0, hd] = ((xk * gain_cos + rope_term) * lax.rsqrt(msk + RMS_EPS)).astype(BF16)
        blk = (vt_all[hd * MLA_VT_ROWS:(hd + 1) * MLA_VT_ROWS, :] + ones).astype(BF16)
        for cc in range(ts // tk):
            vt_ref[0, hd, cc] = blk[:, cc * tk:(cc + 1) * tk]


def _proj_b(x, consts, row_tabs, lane_tabs, kv_tile):
    B, S, _ = x.shape
    ts = min(PROJ_TILE, S)
    tk = min(kv_tile, max(LANES, S // 8))
    rtab = pl.BlockSpec((B_ROPE_DIM // 2, ts), lambda b, s: (0, s))
    ltab = pl.BlockSpec((ts, LANES), lambda b, s: (s, 0))
    return pl.pallas_call(
        _proj_b_kernel,
        grid=(B, S // ts),
        in_specs=[pl.BlockSpec((1, ts, D_MODEL), lambda b, s: (b, s, 0))]
                 + [_resident(a.shape) for a in consts] + [rtab, rtab, ltab, ltab],
        out_specs=[pl.BlockSpec((1, B_HEADS, LANES, ts), lambda b, s: (b, 0, 0, s)),
                   pl.BlockSpec((1, B_HEADS, ts, LANES), lambda b, s: (b, 0, s, 0)),
                   pl.BlockSpec((1, B_HEADS, ts // tk, MLA_VT_ROWS, tk),
                                lambda b, s: (b, 0, s, 0, 0))],
        out_shape=[jax.ShapeDtypeStruct((B, B_HEADS, LANES, S), BF16),
                   jax.ShapeDtypeStruct((B, B_HEADS, S, LANES), BF16),
                   jax.ShapeDtypeStruct((B, B_HEADS, S // tk, MLA_VT_ROWS, tk), BF16)],
        compiler_params=_compiler_params(("parallel", "parallel")),
        name="proj_b",
    )(x, *consts, *row_tabs, *lane_tabs)


def _attn_b_kernel(qt_ref, k_ref, vt_ref, o_ref, s0_ref, s1_ref, m_ref, acc_ref, *,
                   window_chunks):
    n_chunks, tk = vt_ref.shape[2], vt_ref.shape[4]
    m_ref[...] = jnp.full(m_ref.shape, MASKED, F32)
    acc_ref[...] = jnp.zeros(acc_ref.shape, F32)

    def scores(c, s_ref):
        start = pl.multiple_of(c * tk, tk)
        col_max = []
        for hh in range(2):
            s = jnp.dot(k_ref[0, hh, pl.ds(start, tk), :], qt_ref[0, hh],
                        preferred_element_type=F32)
            s_ref[hh] = s
            col_max.append(jnp.max(s, axis=0, keepdims=True))
        return tuple(col_max)

    def accumulate(c, s_ref, col_max):
        for hh in range(2):
            m_old = m_ref[hh]
            m_new = jnp.maximum(m_old, col_max[hh])
            p = jnp.exp2(s_ref[hh] - m_new).astype(BF16)
            acc_ref[hh] = (acc_ref[hh] * jnp.exp2(m_old - m_new)
                           + jnp.dot(vt_ref[0, hh, c], p, preferred_element_type=F32))
            m_ref[hh] = m_new

    bufs = (s0_ref, s1_ref)

    def window(c0, col_max, last):
        for w in range(window_chunks):
            is_final = last and w == window_chunks - 1
            nxt = None if is_final else scores(c0 + w + 1, bufs[(w + 1) % 2])
            accumulate(c0 + w, bufs[w % 2], col_max)
            col_max = nxt
        return col_max

    n_windows = n_chunks // window_chunks
    col_max = lax.fori_loop(0, n_windows - 1,
                            lambda i, mx: window(i * window_chunks, mx, False),
                            scores(0, s0_ref))
    window((n_windows - 1) * window_chunks, col_max, True)
    halves = [acc_ref[hh][:B_V_DIM] / acc_ref[hh][B_V_DIM:B_V_DIM + 1] for hh in range(2)]
    o_ref[0] = jnp.concatenate(halves, axis=0).T.astype(BF16)


def _attn_b(qt, k, vt):
    B, H, _, S = qt.shape
    tq = min(MLA_Q_TILE, S)
    n_chunks, tk = vt.shape[2], vt.shape[4]
    window = min(MLA_WINDOW_CHUNKS, n_chunks)
    assert n_chunks % window == 0 and window % 2 == 0
    score_buf = pltpu.VMEM((2, tk, tq), F32)
    return pl.pallas_call(
        functools.partial(_attn_b_kernel, window_chunks=window),
        grid=(B, H // 2, S // tq),
        in_specs=[pl.BlockSpec((1, 2, LANES, tq), lambda b, j, i: (b, j, 0, i)),
                  pl.BlockSpec((1, 2, S, LANES), lambda b, j, i: (b, j, 0, 0)),
                  pl.BlockSpec((1, 2, n_chunks, MLA_VT_ROWS, tk), lambda b, j, i: (b, j, 0, 0, 0))],
        out_specs=pl.BlockSpec((1, tq, LANES), lambda b, j, i: (b, i, j)),
        out_shape=jax.ShapeDtypeStruct((B, S, H * B_V_DIM), BF16),
        scratch_shapes=[score_buf, score_buf, pltpu.VMEM((2, 1, tq), F32),
                        pltpu.VMEM((2, MLA_VT_ROWS, tq), F32)],
        compiler_params=_compiler_params(("parallel", "parallel", "arbitrary")),
        name="attn_b",
    )(qt, k, vt)


def _post_kernel(x_ref, o_ref, p_ref, wo_ref, fg_ref, wgu_ref, wd_ref, pg_ref, wpg_ref,
                 wpp_ref, out_ref):
    x = x_ref[...] + jnp.dot(o_ref[...], wo_ref[...], preferred_element_type=F32)
    h = _rms(x, fg_ref[...]).astype(BF16)
    gu = jnp.dot(h, wgu_ref[...], preferred_element_type=F32)
    gate, up = gu[:, :FFN_HIDDEN], gu[:, FFN_HIDDEN:]
    act = (gate * _sigmoid(gate) * up).astype(BF16)
    x = x + jnp.dot(act, wd_ref[...], preferred_element_type=F32)
    h = _rms(x, pg_ref[...]).astype(BF16)
    ple_gate = _sigmoid(jnp.dot(h, wpg_ref[...], preferred_element_type=F32))
    emb = jnp.dot(p_ref[0].astype(BF16), wpp_ref[...], preferred_element_type=F32)
    out_ref[...] = x + ple_gate * emb


def _post(x, o, p, layer, wo, fg, wgu, wd, pg, wpg, wpp):
    T = x.shape[0]
    tm = min(POST_TILE, T)
    consts = [wo, fg, wgu, wd, pg, wpg, wpp]
    row = lambda width: pl.BlockSpec((tm, width), lambda t: (t, 0))
    return pl.pallas_call(
        _post_kernel,
        grid=(T // tm,),
        in_specs=[row(D_MODEL), row(o.shape[1]),
                  pl.BlockSpec((1, tm, PLE_DIM), lambda t: (layer, t, 0))]
                 + [_resident(a.shape) for a in consts],
        out_specs=row(D_MODEL),
        out_shape=jax.ShapeDtypeStruct((T, D_MODEL), F32),
        compiler_params=_compiler_params(("parallel",)),
        name="post",
    )(x, o, p, *consts)


def _head_slots(w, heads, width):
    kdim = w.shape[0]
    w = w.reshape(kdim, heads, width)
    return jnp.pad(w, ((0, 0), (0, 0), (0, LANES - width))).reshape(kdim, heads * LANES)


def _lane_replicated(v):
    return jnp.broadcast_to(v[:, None], (v.shape[0], LANES))


def _prep_a(w_qkv, q_gain, k_gain, sink):
    scale = (A_HEAD_DIM ** -0.5) * LOG2E
    n_q = A_HEADS * A_HEAD_DIM
    n_qk = n_q + A_KV_HEADS * A_HEAD_DIM
    w_qt = w_qkv[:, :n_q].T.astype(BF16)
    w_k = w_qkv[:, n_q:n_qk].astype(BF16)
    w_vt = _head_slots(w_qkv[:, n_qk:], A_KV_HEADS, A_HEAD_DIM).T.astype(BF16)
    gq = _lane_replicated(q_gain * scale)
    gk = jnp.tile(k_gain, 2)[None, :]
    group = A_HEADS // A_KV_HEADS
    sink_rows = jnp.repeat(sink.reshape(A_KV_HEADS, group), BLOCK, axis=1)[:, None, :]
    return (w_qt, w_k, w_vt, gq, gk), sink_rows


def _prep_b(w_in, w_uq, w_ukv, q_gain, k_gain, q_lat_gain, kv_lat_gain, g):
    scale = (B_QK_DIM ** -0.5) * LOG2E
    half = B_ROPE_DIM // 2
    lat_cols = B_Q_RANK + B_KV_RANK
    rope_cols = w_in[:, lat_cols:]
    rope_sw_cols = jnp.concatenate([rope_cols[:, half:], rope_cols[:, :half]], axis=1)
    slot = lambda cols: jnp.pad(cols, ((0, 0), (B_NOPE_DIM, LANES - B_QK_DIM)))
    w_in_p = jnp.concatenate([w_in[:, :lat_cols], slot(rope_cols), slot(rope_sw_cols)],
                             axis=1).astype(BF16)
    w_uqt = _head_slots(w_uq, B_HEADS, B_QK_DIM).T.astype(BF16)
    ukv = w_ukv.reshape(B_KV_RANK, B_HEADS, B_NOPE_DIM + B_V_DIM)
    w_uk = _head_slots(ukv[:, :, :B_NOPE_DIM].reshape(B_KV_RANK, -1), B_HEADS, B_NOPE_DIM).astype(BF16)
    w_uv = jnp.pad(ukv[:, :, B_NOPE_DIM:], ((0, 0), (0, 0), (0, MLA_VT_ROWS - B_V_DIM)))
    w_uvt = w_uv.reshape(B_KV_RANK, -1).T.astype(BF16)
    pad = (0, LANES - B_QK_DIM)
    gq = _lane_replicated(jnp.pad(q_gain * scale, pad))
    gk = jnp.pad(k_gain, pad)[None, :]
    k_gain_sw = jnp.concatenate([k_gain[:B_NOPE_DIM], k_gain[B_NOPE_DIM + half:],
                                 k_gain[B_NOPE_DIM:B_NOPE_DIM + half]])
    gk_sw = jnp.pad(k_gain_sw, pad)[None, :]
    return [g, w_in_p, q_lat_gain[None, :], kv_lat_gain[None, :], w_uqt, w_uk, w_uvt,
            gq, gk, gk_sw]


def _trunk(x, p, mix_norm, layers_a, layers_b, a_w_o, b_w_o, ffn_norm, w_gu, w_d,
           ple_norm, w_pg, w_pp):
    B, S, _ = x.shape
    depth = p.shape[0]
    T = B * S
    p = p.reshape(depth, T, PLE_DIM)
    rows_a = _row_tables(S, A_ROT_DIM)
    lanes_a = _lane_tables(S, A_ROT_DIM, (0, HALF))
    rows_b = _row_tables(S, B_ROPE_DIM)
    c_b, lo_b, hi_b = _lane_tables(S, B_ROPE_DIM, (B_NOPE_DIM,))
    lanes_b = (c_b, lo_b + hi_b)
    for i in range(depth):
        j = i // 2
        g = mix_norm[i][None, :]
        if i % 2 == 0:
            consts, sink_rows = layers_a[j]
            qt, k, vt = _proj_a(x, g, *consts, rows_a, lanes_a)
            o = _attn_a(sink_rows, qt, k, vt)
            w_o = a_w_o[j]
        else:
            qt, k, vt = _proj_b(x, layers_b[j](g), rows_b, lanes_b,
                                MLA_KV_TILES[j % len(MLA_KV_TILES)])
            o = _attn_b(qt, k, vt)
            w_o = b_w_o[j]
        x = _post(x.reshape(T, D_MODEL), o.reshape(T, -1), p, i, w_o, ffn_norm[i][None, :],
                  w_gu[i], w_d[i], ple_norm[i][None, :], w_pg[i], w_pp[i]).reshape(B, S, D_MODEL)
    return x


def kernel(x_prompt, x_sample, p_prompt, p_sample, mix_norm, a_w_qkv, a_q_norm, a_k_norm, a_sink, a_w_o, b_w_in, b_q_lat_norm, b_kv_lat_norm, b_w_uq, b_w_ukv, b_q_norm, b_k_norm, b_w_o, ffn_norm, ffn_w_gate_up, ffn_w_down, ple_norm, ple_w_gate, ple_w_proj):
    layers_a = [_prep_a(a_w_qkv[j], a_q_norm[j], a_k_norm[j], a_sink[j])
                for j in range(a_w_qkv.shape[0])]
    layers_b = [functools.partial(_prep_b, b_w_in[j], b_w_uq[j], b_w_ukv[j], b_q_norm[j],
                                  b_k_norm[j], b_q_lat_norm[j], b_kv_lat_norm[j])
                for j in range(b_w_in.shape[0])]
    shared = (mix_norm, layers_a, layers_b, a_w_o.astype(BF16), b_w_o.astype(BF16),
              ffn_norm, ffn_w_gate_up.astype(BF16), ffn_w_down.astype(BF16), ple_norm,
              ple_w_gate.astype(BF16), ple_w_proj.astype(BF16))
    return (_trunk(x_prompt, p_prompt, *shared), _trunk(x_sample, p_sample, *shared))
```

```python
import functools
import math

import jax
import jax.numpy as jnp
from jax import lax
from jax.experimental import pallas as pl
from jax.experimental.pallas import tpu as pltpu

F32 = jnp.float32
BF16 = jnp.bfloat16

D_MODEL = 1024
PLE_DIM = 256
RMS_EPS = 1e-6
ROPE_THETA = 500000.0
BLOCK = 128
A_HEADS = 16
A_KV_HEADS = 4
A_HEAD_DIM = 64
A_ROT_DIM = 16
B_HEADS = 16
B_Q_RANK = 384
B_KV_RANK = 128
B_NOPE_DIM = 64
B_ROPE_DIM = 32
B_V_DIM = 64
B_QK_DIM = B_NOPE_DIM + B_ROPE_DIM
FFN_HIDDEN = 2816

LANES = 128
HALF = LANES // 2
VMEM_LIMIT_BYTES = 56 * 1024 * 1024

LOG2E = math.log2(math.e)
MASKED = -1e30
NT_DIMS = (((1,), (1,)), ((), ()))

PROJ_TILE = 512
POST_TILE = 512
MLA_Q_TILE = 512
MLA_KV_TILE = 512
MLA_WINDOW_CHUNKS = 4
MLA_VT_ROWS = 80
ATTN_A_BLOCKS_PER_STEP = 16


def _compiler_params(semantics):
    return pltpu.CompilerParams(dimension_semantics=semantics,
                                vmem_limit_bytes=VMEM_LIMIT_BYTES)


def _resident(shape):
    nd = len(shape)
    return pl.BlockSpec(shape, lambda *_: (0,) * nd, pipeline_mode=pl.Buffered(1))


def _rms(x, g):
    ms = jnp.mean(x * x, axis=-1, keepdims=True)
    return x * lax.rsqrt(ms + RMS_EPS) * g


def _sigmoid(x):
    return 1.0 / (1.0 + jnp.exp(-x))


def _lane_iota(shape):
    return lax.broadcasted_iota(jnp.int32, shape, len(shape) - 1)


def _across_lanes(col_block, n):
    return jnp.concatenate([col_block] * (n // LANES), axis=1)


def _rope_rows(x1, x2, cos_t, sin_t):
    return x1 * cos_t - x2 * sin_t, x1 * sin_t + x2 * cos_t


def _rope_lanes(x, c, s_from_below, s_from_above, shift):
    return (x * c + pltpu.roll(x, shift, 1) * s_from_below
            + pltpu.roll(x, LANES - shift, 1) * s_from_above)


def _angles(seq, rot_dim):
    inv = 1.0 / (ROPE_THETA ** (jnp.arange(0, rot_dim, 2, dtype=F32) / rot_dim))
    ang = jnp.arange(seq, dtype=F32)[:, None] * inv[None, :]
    return jnp.cos(ang), jnp.sin(ang)


def _lane_tables(seq, rot_dim, lane_offsets):
    half = rot_dim // 2
    cos, sin = _angles(seq, rot_dim)

    def place(block, off):
        return jnp.pad(block, ((0, 0), (off, LANES - off - half)))

    c = jnp.zeros((seq, LANES), F32)
    rotated = jnp.zeros((seq, LANES), F32)
    lo = jnp.zeros((seq, LANES), F32)
    hi = jnp.zeros((seq, LANES), F32)
    for off in lane_offsets:
        c = c + place(cos, off) + place(cos, off + half)
        rotated = rotated + place(jnp.ones_like(cos), off) + place(jnp.ones_like(cos), off + half)
        hi = hi + place(-sin, off)
        lo = lo + place(sin, off + half)
    return c + (1.0 - rotated), lo, hi


def _row_tables(seq, rot_dim):
    cos, sin = _angles(seq, rot_dim)
    return cos.T, sin.T


def _ones_row(shape):
    return (lax.broadcasted_iota(jnp.int32, shape, 0) == B_V_DIM).astype(F32)


def _proj_a_kernel(x_ref, g_ref, wqt_ref, wk_ref, wvt_ref, gq_ref, gk_ref, ct_ref, st_ref,
                   c_ref, sl_ref, sh_ref, q_ref, k_ref, vt_ref):
    x = x_ref[0]
    ts = x.shape[0]
    h = _rms(x, g_ref[...]).astype(BF16)
    q_t = lax.dot_general(wqt_ref[...], h, NT_DIMS, preferred_element_type=F32)
    kk = jnp.dot(h, wk_ref[...], preferred_element_type=F32)
    vt = lax.dot_general(wvt_ref[...], h, NT_DIMS, preferred_element_type=F32)

    gq = _across_lanes(gq_ref[...], ts)
    cos_t, sin_t = ct_ref[...], st_ref[...]
    half = A_ROT_DIM // 2
    pad = jnp.zeros((LANES - A_HEAD_DIM, BLOCK), BF16)
    for hd in range(A_HEADS):
        t = q_t[hd * A_HEAD_DIM:(hd + 1) * A_HEAD_DIM, :]
        ms = jnp.sum(t * t, axis=0, keepdims=True) / A_HEAD_DIM
        tn = t * lax.rsqrt(ms + RMS_EPS) * gq
        o1, o2 = _rope_rows(tn[:half], tn[half:A_ROT_DIM], cos_t, sin_t)
        out = jnp.concatenate([o1, o2, tn[A_ROT_DIM:]], axis=0).astype(BF16)
        for cc in range(ts // BLOCK):
            q_ref[0, hd, cc, :A_HEAD_DIM, :] = out[:, cc * BLOCK:(cc + 1) * BLOCK]
            q_ref[0, hd, cc, A_HEAD_DIM:, :] = pad

    c, sl, sh = c_ref[...], sl_ref[...], sh_ref[...]
    is_lo = _lane_iota((ts, LANES)) < HALF
    for j in range(A_KV_HEADS * A_HEAD_DIM // LANES):
        t = kk[:, j * LANES:(j + 1) * LANES]
        sq = t * t
        ss_lo = jnp.sum(jnp.where(is_lo, sq, 0.0), axis=-1, keepdims=True)
        ss_hi = jnp.sum(jnp.where(is_lo, 0.0, sq), axis=-1, keepdims=True)
        r = jnp.where(is_lo, lax.rsqrt(ss_lo / A_HEAD_DIM + RMS_EPS),
                      lax.rsqrt(ss_hi / A_HEAD_DIM + RMS_EPS))
        t = _rope_lanes(t * r * gk_ref[...], c, sl, sh, half)
        k_ref[0, 2 * j] = jnp.where(is_lo, t, 0.0).astype(BF16)
        k_ref[0, 2 * j + 1] = jnp.where(is_lo, pltpu.roll(t, HALF, 1), 0.0).astype(BF16)

    ones = _ones_row((LANES, ts))
    for g in range(A_KV_HEADS):
        blk = (vt[g * LANES:(g + 1) * LANES, :] + ones).astype(BF16)
        for cc in range(ts // BLOCK):
            vt_ref[0, g, cc] = blk[:, cc * BLOCK:(cc + 1) * BLOCK]


def _proj_a(x, g, w_qt, w_k, w_vt, gq, gk, row_tabs, lane_tabs):
    B, S, _ = x.shape
    ts = min(PROJ_TILE, S)
    rtab = pl.BlockSpec((A_ROT_DIM // 2, ts), lambda b, s: (0, s))
    ltab = pl.BlockSpec((ts, LANES), lambda b, s: (s, 0))
    consts = [g, w_qt, w_k, w_vt, gq, gk]
    return pl.pallas_call(
        _proj_a_kernel,
        grid=(B, S // ts),
        in_specs=[pl.BlockSpec((1, ts, D_MODEL), lambda b, s: (b, s, 0))]
                 + [_resident(a.shape) for a in consts] + [rtab, rtab, ltab, ltab, ltab],
        out_specs=[pl.BlockSpec((1, A_HEADS, ts // BLOCK, LANES, BLOCK),
                                lambda b, s: (b, 0, s, 0, 0)),
                   pl.BlockSpec((1, A_KV_HEADS, ts, LANES), lambda b, s: (b, 0, s, 0)),
                   pl.BlockSpec((1, A_KV_HEADS, ts // BLOCK, LANES, BLOCK),
                                lambda b, s: (b, 0, s, 0, 0))],
        out_shape=[jax.ShapeDtypeStruct((B, A_HEADS, S // BLOCK, LANES, BLOCK), BF16),
                   jax.ShapeDtypeStruct((B, A_KV_HEADS, S, LANES), BF16),
                   jax.ShapeDtypeStruct((B, A_KV_HEADS, S // BLOCK, LANES, BLOCK), BF16)],
        compiler_params=_compiler_params(("parallel", "parallel")),
        name="proj_a",
    )(x, *consts, *row_tabs, *lane_tabs)


def _attn_a_kernel(sink_ref, qt_ref, k_ref, vt_ref, o_ref, s0_ref, s1_ref, *, blocks_per_step):
    step = pl.program_id(2)
    nb = vt_ref.shape[2]
    group = A_HEADS // A_KV_HEADS
    width = group * BLOCK
    key = lax.broadcasted_iota(jnp.int32, (BLOCK, width), 0)
    qry = lax.broadcasted_iota(jnp.int32, (BLOCK, width), 1) & (BLOCK - 1)
    sink = sink_ref[0] * LOG2E

    def neighbours(r):
        n = step * blocks_per_step + r
        return jnp.maximum(n - 1, 0), n, jnp.minimum(n + 1, nb - 1)

    def scores(r, s_ref):
        i_prev, n, i_next = neighbours(r)
        q4t = jnp.concatenate([qt_ref[0, hd, r] for hd in range(group)], axis=1)

        def block_scores(i):
            k = k_ref[0, 0, pl.ds(pl.multiple_of(i * BLOCK, BLOCK), BLOCK), :]
            return jnp.dot(k, q4t, preferred_element_type=F32)

        s_p = jnp.where((key >= qry) & (n > 0), block_scores(i_prev), MASKED)
        s_c = block_scores(n)
        s_n = jnp.where((key <= qry) & (n < nb - 1), block_scores(i_next), MASKED)
        s_ref[0], s_ref[1], s_ref[2] = s_p, s_c, s_n
        m = jnp.max(jnp.maximum(jnp.maximum(s_p, s_c), s_n), axis=0, keepdims=True)
        return jnp.maximum(m, sink)

    def finish(r, s_ref, m):
        acc = jnp.zeros((LANES, width), F32)
        for x, i in enumerate(neighbours(r)):
            acc += jnp.dot(vt_ref[0, 0, i], jnp.exp2(s_ref[x] - m).astype(BF16),
                           preferred_element_type=F32)
        denom = acc[B_V_DIM:B_V_DIM + 1] + jnp.exp2(sink - m)
        o = acc[:B_V_DIM] / denom
        rows = pl.ds(pl.multiple_of(r * BLOCK, BLOCK), BLOCK)
        for pi in range(group // 2):
            pair = jnp.concatenate([o[:, (2 * pi) * BLOCK:(2 * pi + 1) * BLOCK],
                                    o[:, (2 * pi + 1) * BLOCK:(2 * pi + 2) * BLOCK]], axis=0)
            o_ref[0, rows, pi * LANES:(pi + 1) * LANES] = pair.T.astype(BF16)

    def pair_of_blocks(i, m0, last):
        r = 2 * i
        m1 = scores(r + 1, s1_ref)
        finish(r, s0_ref, m0)
        m0 = None if last else scores(r + 2, s0_ref)
        finish(r + 1, s1_ref, m1)
        return m0

    n_pairs = blocks_per_step // 2
    m0 = lax.fori_loop(0, n_pairs - 1, lambda i, m: pair_of_blocks(i, m, False),
                       scores(0, s0_ref))
    pair_of_blocks(n_pairs - 1, m0, True)


def _attn_a(sink_rows, qt, k, vt):
    B, _, nb, _, _ = qt.shape
    S = nb * BLOCK
    r = min(ATTN_A_BLOCKS_PER_STEP, nb)
    assert r % 2 == 0 and nb % r == 0
    group = A_HEADS // A_KV_HEADS
    score_buf = pltpu.VMEM((3, BLOCK, group * BLOCK), F32)
    return pl.pallas_call(
        functools.partial(_attn_a_kernel, blocks_per_step=r),
        grid=(B, A_KV_HEADS, nb // r),
        in_specs=[pl.BlockSpec((1, 1, group * BLOCK), lambda b, g, n: (g, 0, 0)),
                  pl.BlockSpec((1, group, r, LANES, BLOCK), lambda b, g, n: (b, g, n, 0, 0)),
                  pl.BlockSpec((1, 1, S, LANES), lambda b, g, n: (b, g, 0, 0)),
                  pl.BlockSpec((1, 1, nb, LANES, BLOCK), lambda b, g, n: (b, g, 0, 0, 0))],
        out_specs=pl.BlockSpec((1, r * BLOCK, group * A_HEAD_DIM), lambda b, g, n: (b, n, g)),
        out_shape=jax.ShapeDtypeStruct((B, S, A_HEADS * A_HEAD_DIM), BF16),
        scratch_shapes=[score_buf, score_buf],
        compiler_params=_compiler_params(("parallel", "parallel", "arbitrary")),
        name="attn_a",
    )(sink_rows, qt, k, vt)


def _proj_b_kernel(x_ref, g_ref, win_ref, gql_ref, gkvl_ref, wuqt_ref, wuk_ref, wuvt_ref,
                   gq_ref, gk_ref, gksw_ref, ct_ref, st_ref, c_ref, ssg_ref,
                   q_ref, k_ref, vt_ref):
    x = x_ref[0]
    ts = x.shape[0]
    h = _rms(x, g_ref[...]).astype(BF16)
    lat = jnp.dot(h, win_ref[...], preferred_element_type=F32)
    cq = _rms(lat[:, :B_Q_RANK], gql_ref[...]).astype(BF16)
    ckv = _rms(lat[:, B_Q_RANK:B_Q_RANK + B_KV_RANK], gkvl_ref[...]).astype(BF16)
    rope_off = B_Q_RANK + B_KV_RANK
    k_rope = lat[:, rope_off:rope_off + LANES]
    k_rope_sw = lat[:, rope_off + LANES:]
    q_t = lax.dot_general(wuqt_ref[...], cq, NT_DIMS, preferred_element_type=F32)
    kn_all = jnp.dot(ckv, wuk_ref[...], preferred_element_type=F32)
    vt_all = lax.dot_general(wuvt_ref[...], ckv, NT_DIMS, preferred_element_type=F32)

    gq = _across_lanes(gq_ref[...], ts)
    cos_t, sin_t = ct_ref[...], st_ref[...]
    half = B_ROPE_DIM // 2
    gain_cos = c_ref[...] * gk_ref[...]
    rope_term = k_rope_sw * (ssg_ref[...] * gksw_ref[...])
    ones = _ones_row((MLA_VT_ROWS, ts))
    tk = vt_ref.shape[4]
    for hd in range(B_HEADS):
        sl_h = slice(hd * LANES, (hd + 1) * LANES)
        t = q_t[sl_h, :]
        ms = jnp.sum(t * t, axis=0, keepdims=True) / B_QK_DIM
        tn = t * lax.rsqrt(ms + RMS_EPS) * gq
        o1, o2 = _rope_rows(tn[B_NOPE_DIM:B_NOPE_DIM + half], tn[B_NOPE_DIM + half:B_QK_DIM],
                            cos_t, sin_t)
        q_ref[0, hd] = jnp.concatenate([tn[:B_NOPE_DIM], o1, o2, tn[B_QK_DIM:]],
                                       axis=0).astype(BF16)
        xk = kn_all[:, sl_h] + k_rope
        msk = jnp.sum(xk * xk, axis=-1, keepdims=True) / B_QK_DIM
        k_ref[0, hd] = ((xk * gain_cos + rope_term) * lax.rsqrt(msk + RMS_EPS)).astype(BF16)
        blk = (vt_all[hd * MLA_VT_ROWS:(hd + 1) * MLA_VT_ROWS, :] + ones).astype(BF16)
        for cc in range(ts // tk):
            vt_ref[0, hd, cc] = blk[:, cc * tk:(cc + 1) * tk]


def _proj_b(x, consts, row_tabs, lane_tabs):
    B, S, _ = x.shape
    ts = min(PROJ_TILE, S)
    tk = min(MLA_KV_TILE, max(LANES, S // 8))
    rtab = pl.BlockSpec((B_ROPE_DIM // 2, ts), lambda b, s: (0, s))
    ltab = pl.BlockSpec((ts, LANES), lambda b, s: (s, 0))
    return pl.pallas_call(
        _proj_b_kernel,
        grid=(B, S // ts),
        in_specs=[pl.BlockSpec((1, ts, D_MODEL), lambda b, s: (b, s, 0))]
                 + [_resident(a.shape) for a in consts] + [rtab, rtab, ltab, ltab],
        out_specs=[pl.BlockSpec((1, B_HEADS, LANES, ts), lambda b, s: (b, 0, 0, s)),
                   pl.BlockSpec((1, B_HEADS, ts, LANES), lambda b, s: (b, 0, s, 0)),
                   pl.BlockSpec((1, B_HEADS, ts // tk, MLA_VT_ROWS, tk),
                                lambda b, s: (b, 0, s, 0, 0))],
        out_shape=[jax.ShapeDtypeStruct((B, B_HEADS, LANES, S), BF16),
                   jax.ShapeDtypeStruct((B, B_HEADS, S, LANES), BF16),
                   jax.ShapeDtypeStruct((B, B_HEADS, S // tk, MLA_VT_ROWS, tk), BF16)],
        compiler_params=_compiler_params(("parallel", "parallel")),
        name="proj_b",
    )(x, *consts, *row_tabs, *lane_tabs)


def _attn_b_kernel(qt_ref, k_ref, vt_ref, o_ref, s0_ref, s1_ref, m_ref, acc_ref, *,
                   window_chunks):
    n_chunks, tk = vt_ref.shape[2], vt_ref.shape[4]
    m_ref[...] = jnp.full(m_ref.shape, MASKED, F32)
    acc_ref[...] = jnp.zeros(acc_ref.shape, F32)

    def scores(c, s_ref):
        start = pl.multiple_of(c * tk, tk)
        col_max = []
        for hh in range(2):
            s = jnp.dot(k_ref[0, hh, pl.ds(start, tk), :], qt_ref[0, hh],
                        preferred_element_type=F32)
            s_ref[hh] = s
            col_max.append(jnp.max(s, axis=0, keepdims=True))
        return tuple(col_max)

    def accumulate(c, s_ref, col_max):
        for hh in range(2):
            m_old = m_ref[hh]
            m_new = jnp.maximum(m_old, col_max[hh])
            p = jnp.exp2(s_ref[hh] - m_new).astype(BF16)
            acc_ref[hh] = (acc_ref[hh] * jnp.exp2(m_old - m_new)
                           + jnp.dot(vt_ref[0, hh, c], p, preferred_element_type=F32))
            m_ref[hh] = m_new

    bufs = (s0_ref, s1_ref)

    def window(c0, col_max, last):
        for w in range(window_chunks):
            is_final = last and w == window_chunks - 1
            nxt = None if is_final else scores(c0 + w + 1, bufs[(w + 1) % 2])
            accumulate(c0 + w, bufs[w % 2], col_max)
            col_max = nxt
        return col_max

    n_windows = n_chunks // window_chunks
    col_max = lax.fori_loop(0, n_windows - 1,
                            lambda i, mx: window(i * window_chunks, mx, False),
                            scores(0, s0_ref))
    window((n_windows - 1) * window_chunks, col_max, True)
    halves = [acc_ref[hh][:B_V_DIM] / acc_ref[hh][B_V_DIM:B_V_DIM + 1] for hh in range(2)]
    o_ref[0] = jnp.concatenate(halves, axis=0).T.astype(BF16)


def _attn_b(qt, k, vt):
    B, H, _, S = qt.shape
    tq = min(MLA_Q_TILE, S)
    n_chunks, tk = vt.shape[2], vt.shape[4]
    window = min(MLA_WINDOW_CHUNKS, n_chunks)
    assert n_chunks % window == 0 and window % 2 == 0
    score_buf = pltpu.VMEM((2, tk, tq), F32)
    return pl.pallas_call(
        functools.partial(_attn_b_kernel, window_chunks=window),
        grid=(B, H // 2, S // tq),
        in_specs=[pl.BlockSpec((1, 2, LANES, tq), lambda b, j, i: (b, j, 0, i)),
                  pl.BlockSpec((1, 2, S, LANES), lambda b, j, i: (b, j, 0, 0)),
                  pl.BlockSpec((1, 2, n_chunks, MLA_VT_ROWS, tk), lambda b, j, i: (b, j, 0, 0, 0))],
        out_specs=pl.BlockSpec((1, tq, LANES), lambda b, j, i: (b, i, j)),
        out_shape=jax.ShapeDtypeStruct((B, S, H * B_V_DIM), BF16),
        scratch_shapes=[score_buf, score_buf, pltpu.VMEM((2, 1, tq), F32),
                        pltpu.VMEM((2, MLA_VT_ROWS, tq), F32)],
        compiler_params=_compiler_params(("parallel", "parallel", "arbitrary")),
        name="attn_b",
    )(qt, k, vt)


def _post_kernel(x_ref, o_ref, p_ref, wo_ref, fg_ref, wgu_ref, wd_ref, pg_ref, wpg_ref,
                 wpp_ref, out_ref):
    x = x_ref[...] + jnp.dot(o_ref[...], wo_ref[...], preferred_element_type=F32)
    h = _rms(x, fg_ref[...]).astype(BF16)
    gu = jnp.dot(h, wgu_ref[...], preferred_element_type=F32)
    gate, up = gu[:, :FFN_HIDDEN], gu[:, FFN_HIDDEN:]
    act = (gate * _sigmoid(gate) * up).astype(BF16)
    x = x + jnp.dot(act, wd_ref[...], preferred_element_type=F32)
    h = _rms(x, pg_ref[...]).astype(BF16)
    ple_gate = _sigmoid(jnp.dot(h, wpg_ref[...], preferred_element_type=F32))
    emb = jnp.dot(p_ref[0].astype(BF16), wpp_ref[...], preferred_element_type=F32)
    out_ref[...] = x + ple_gate * emb


def _post(x, o, p, layer, wo, fg, wgu, wd, pg, wpg, wpp):
    T = x.shape[0]
    tm = min(POST_TILE, T)
    consts = [wo, fg, wgu, wd, pg, wpg, wpp]
    row = lambda width: pl.BlockSpec((tm, width), lambda t: (t, 0))
    return pl.pallas_call(
        _post_kernel,
        grid=(T // tm,),
        in_specs=[row(D_MODEL), row(o.shape[1]),
                  pl.BlockSpec((1, tm, PLE_DIM), lambda t: (layer, t, 0))]
                 + [_resident(a.shape) for a in consts],
        out_specs=row(D_MODEL),
        out_shape=jax.ShapeDtypeStruct((T, D_MODEL), F32),
        compiler_params=_compiler_params(("parallel",)),
        name="post",
    )(x, o, p, *consts)


def _head_slots(w, heads, width):
    kdim = w.shape[0]
    w = w.reshape(kdim, heads, width)
    return jnp.pad(w, ((0, 0), (0, 0), (0, LANES - width))).reshape(kdim, heads * LANES)


def _lane_replicated(v):
    return jnp.broadcast_to(v[:, None], (v.shape[0], LANES))


def _prep_a(w_qkv, q_gain, k_gain, sink):
    scale = (A_HEAD_DIM ** -0.5) * LOG2E
    n_q = A_HEADS * A_HEAD_DIM
    n_qk = n_q + A_KV_HEADS * A_HEAD_DIM
    w_qt = w_qkv[:, :n_q].T.astype(BF16)
    w_k = w_qkv[:, n_q:n_qk].astype(BF16)
    w_vt = _head_slots(w_qkv[:, n_qk:], A_KV_HEADS, A_HEAD_DIM).T.astype(BF16)
    gq = _lane_replicated(q_gain * scale)
    gk = jnp.tile(k_gain, 2)[None, :]
    group = A_HEADS // A_KV_HEADS
    sink_rows = jnp.repeat(sink.reshape(A_KV_HEADS, group), BLOCK, axis=1)[:, None, :]
    return (w_qt, w_k, w_vt, gq, gk), sink_rows


def _prep_b(w_in, w_uq, w_ukv, q_gain, k_gain, q_lat_gain, kv_lat_gain, g):
    scale = (B_QK_DIM ** -0.5) * LOG2E
    half = B_ROPE_DIM // 2
    lat_cols = B_Q_RANK + B_KV_RANK
    rope_cols = w_in[:, lat_cols:]
    rope_sw_cols = jnp.concatenate([rope_cols[:, half:], rope_cols[:, :half]], axis=1)
    slot = lambda cols: jnp.pad(cols, ((0, 0), (B_NOPE_DIM, LANES - B_QK_DIM)))
    w_in_p = jnp.concatenate([w_in[:, :lat_cols], slot(rope_cols), slot(rope_sw_cols)],
                             axis=1).astype(BF16)
    w_uqt = _head_slots(w_uq, B_HEADS, B_QK_DIM).T.astype(BF16)
    ukv = w_ukv.reshape(B_KV_RANK, B_HEADS, B_NOPE_DIM + B_V_DIM)
    w_uk = _head_slots(ukv[:, :, :B_NOPE_DIM].reshape(B_KV_RANK, -1), B_HEADS, B_NOPE_DIM).astype(BF16)
    w_uv = jnp.pad(ukv[:, :, B_NOPE_DIM:], ((0, 0), (0, 0), (0, MLA_VT_ROWS - B_V_DIM)))
    w_uvt = w_uv.reshape(B_KV_RANK, -1).T.astype(BF16)
    pad = (0, LANES - B_QK_DIM)
    gq = _lane_replicated(jnp.pad(q_gain * scale, pad))
    gk = jnp.pad(k_gain, pad)[None, :]
    k_gain_sw = jnp.concatenate([k_gain[:B_NOPE_DIM], k_gain[B_NOPE_DIM + half:],
                                 k_gain[B_NOPE_DIM:B_NOPE_DIM + half]])
    gk_sw = jnp.pad(k_gain_sw, pad)[None, :]
    return [g, w_in_p, q_lat_gain[None, :], kv_lat_gain[None, :], w_uqt, w_uk, w_uvt,
            gq, gk, gk_sw]


def _trunk(x, p, mix_norm, layers_a, layers_b, a_w_o, b_w_o, ffn_norm, w_gu, w_d,
           ple_norm, w_pg, w_pp):
    B, S, _ = x.shape
    depth = p.shape[0]
    T = B * S
    p = p.reshape(depth, T, PLE_DIM)
    rows_a = _row_tables(S, A_ROT_DIM)
    lanes_a = _lane_tables(S, A_ROT_DIM, (0, HALF))
    rows_b = _row_tables(S, B_ROPE_DIM)
    c_b, lo_b, hi_b = _lane_tables(S, B_ROPE_DIM, (B_NOPE_DIM,))
    lanes_b = (c_b, lo_b + hi_b)
    for i in range(depth):
        j = i // 2
        g = mix_norm[i][None, :]
        if i % 2 == 0:
            consts, sink_rows = layers_a[j]
            qt, k, vt = _proj_a(x, g, *consts, rows_a, lanes_a)
            o = _attn_a(sink_rows, qt, k, vt)
            w_o = a_w_o[j]
        else:
            qt, k, vt = _proj_b(x, layers_b[j](g), rows_b, lanes_b)
            o = _attn_b(qt, k, vt)
            w_o = b_w_o[j]
        x = _post(x.reshape(T, D_MODEL), o.reshape(T, -1), p, i, w_o, ffn_norm[i][None, :],
                  w_gu[i], w_d[i], ple_norm[i][None, :], w_pg[i], w_pp[i]).reshape(B, S, D_MODEL)
    return x


def kernel(x_prompt, x_sample, p_prompt, p_sample, mix_norm, a_w_qkv, a_q_norm, a_k_norm, a_sink, a_w_o, b_w_in, b_q_lat_norm, b_kv_lat_norm, b_w_uq, b_w_ukv, b_q_norm, b_k_norm, b_w_o, ffn_norm, ffn_w_gate_up, ffn_w_down, ple_norm, ple_w_gate, ple_w_proj):
    layers_a = [_prep_a(a_w_qkv[j], a_q_norm[j], a_k_norm[j], a_sink[j])
                for j in range(a_w_qkv.shape[0])]
    layers_b = [functools.partial(_prep_b, b_w_in[j], b_w_uq[j], b_w_ukv[j], b_q_norm[j],
                                  b_k_norm[j], b_q_lat_norm[j], b_kv_lat_norm[j])
                for j in range(b_w_in.shape[0])]
    shared = (mix_norm, layers_a, layers_b, a_w_o.astype(BF16), b_w_o.astype(BF16),
              ffn_norm, ffn_w_gate_up.astype(BF16), ffn_w_down.astype(BF16), ple_norm,
              ple_w_gate.astype(BF16), ple_w_proj.astype(BF16))
    return (_trunk(x_prompt, p_prompt, *shared), _trunk(x_sample, p_sample, *shared))
```

```python
import functools
import math

import jax
import jax.numpy as jnp
from jax import lax
from jax.experimental import pallas as pl
from jax.experimental.pallas import tpu as pltpu

F32 = jnp.float32
BF16 = jnp.bfloat16

D_MODEL = 1024
PLE_DIM = 256
RMS_EPS = 1e-6
ROPE_THETA = 500000.0
BLOCK = 128
A_HEADS = 16
A_KV_HEADS = 4
A_HEAD_DIM = 64
A_ROT_DIM = 16
B_HEADS = 16
B_Q_RANK = 384
B_KV_RANK = 128
B_NOPE_DIM = 64
B_ROPE_DIM = 32
B_V_DIM = 64
B_QK_DIM = B_NOPE_DIM + B_ROPE_DIM
FFN_HIDDEN = 2816

LANES = 128
HALF = LANES // 2
VMEM_LIMIT_BYTES = 56 * 1024 * 1024

LOG2E = math.log2(math.e)
MASKED = -1e30
NT_DIMS = (((1,), (1,)), ((), ()))

PROJ_TILE = 512
POST_TILE = 512
MLA_Q_TILE = 512
MLA_TILES_PER_STEP = 4
MLA_KV_TILE = 512
MLA_WINDOW_CHUNKS = 4
MLA_VT_ROWS = 80
ATTN_A_BLOCKS_PER_STEP = 64


def _compiler_params(semantics):
    return pltpu.CompilerParams(dimension_semantics=semantics,
                                vmem_limit_bytes=VMEM_LIMIT_BYTES)


def _resident(shape):
    nd = len(shape)
    return pl.BlockSpec(shape, lambda *_: (0,) * nd, pipeline_mode=pl.Buffered(1))


def _rms(x, g):
    ms = jnp.mean(x * x, axis=-1, keepdims=True)
    return x * lax.rsqrt(ms + RMS_EPS) * g


def _sigmoid(x):
    return 1.0 / (1.0 + jnp.exp(-x))


def _lane_iota(shape):
    return lax.broadcasted_iota(jnp.int32, shape, len(shape) - 1)


def _across_lanes(col_block, n):
    return jnp.concatenate([col_block] * (n // LANES), axis=1)


def _rope_rows(x1, x2, cos_t, sin_t):
    return x1 * cos_t - x2 * sin_t, x1 * sin_t + x2 * cos_t


def _rope_lanes(x, c, s_from_below, s_from_above, shift):
    return (x * c + pltpu.roll(x, shift, 1) * s_from_below
            + pltpu.roll(x, LANES - shift, 1) * s_from_above)


def _angles(seq, rot_dim):
    inv = 1.0 / (ROPE_THETA ** (jnp.arange(0, rot_dim, 2, dtype=F32) / rot_dim))
    ang = jnp.arange(seq, dtype=F32)[:, None] * inv[None, :]
    return jnp.cos(ang), jnp.sin(ang)


def _lane_tables(seq, rot_dim, lane_offsets):
    half = rot_dim // 2
    cos, sin = _angles(seq, rot_dim)

    def place(block, off):
        return jnp.pad(block, ((0, 0), (off, LANES - off - half)))

    c = jnp.zeros((seq, LANES), F32)
    rotated = jnp.zeros((seq, LANES), F32)
    lo = jnp.zeros((seq, LANES), F32)
    hi = jnp.zeros((seq, LANES), F32)
    for off in lane_offsets:
        c = c + place(cos, off) + place(cos, off + half)
        rotated = rotated + place(jnp.ones_like(cos), off) + place(jnp.ones_like(cos), off + half)
        hi = hi + place(-sin, off)
        lo = lo + place(sin, off + half)
    return c + (1.0 - rotated), lo, hi


def _row_tables(seq, rot_dim):
    cos, sin = _angles(seq, rot_dim)
    return cos.T, sin.T


def _ones_row(shape):
    return (lax.broadcasted_iota(jnp.int32, shape, 0) == B_V_DIM).astype(F32)


def _proj_a_kernel(x_ref, g_ref, wqt_ref, wk_ref, wvt_ref, gq_ref, gk_ref, ct_ref, st_ref,
                   c_ref, sl_ref, sh_ref, q_ref, k_ref, vt_ref):
    x = x_ref[0]
    ts = x.shape[0]
    h = _rms(x, g_ref[...]).astype(BF16)
    q_t = lax.dot_general(wqt_ref[...], h, NT_DIMS, preferred_element_type=F32)
    kk = jnp.dot(h, wk_ref[...], preferred_element_type=F32)
    vt = lax.dot_general(wvt_ref[...], h, NT_DIMS, preferred_element_type=F32)

    gq = _across_lanes(gq_ref[...], ts)
    cos_t, sin_t = ct_ref[...], st_ref[...]
    half = A_ROT_DIM // 2
    pad = jnp.zeros((LANES - A_HEAD_DIM, BLOCK), BF16)
    for hd in range(A_HEADS):
        t = q_t[hd * A_HEAD_DIM:(hd + 1) * A_HEAD_DIM, :]
        ms = jnp.sum(t * t, axis=0, keepdims=True) / A_HEAD_DIM
        tn = t * lax.rsqrt(ms + RMS_EPS) * gq
        o1, o2 = _rope_rows(tn[:half], tn[half:A_ROT_DIM], cos_t, sin_t)
        out = jnp.concatenate([o1, o2, tn[A_ROT_DIM:]], axis=0).astype(BF16)
        for cc in range(ts // BLOCK):
            q_ref[0, hd, cc, :A_HEAD_DIM, :] = out[:, cc * BLOCK:(cc + 1) * BLOCK]
            q_ref[0, hd, cc, A_HEAD_DIM:, :] = pad

    c, sl, sh = c_ref[...], sl_ref[...], sh_ref[...]
    is_lo = _lane_iota((ts, LANES)) < HALF
    for j in range(A_KV_HEADS * A_HEAD_DIM // LANES):
        t = kk[:, j * LANES:(j + 1) * LANES]
        sq = t * t
        ss_lo = jnp.sum(jnp.where(is_lo, sq, 0.0), axis=-1, keepdims=True)
        ss_hi = jnp.sum(jnp.where(is_lo, 0.0, sq), axis=-1, keepdims=True)
        r = jnp.where(is_lo, lax.rsqrt(ss_lo / A_HEAD_DIM + RMS_EPS),
                      lax.rsqrt(ss_hi / A_HEAD_DIM + RMS_EPS))
        t = _rope_lanes(t * r * gk_ref[...], c, sl, sh, half)
        k_ref[0, 2 * j] = jnp.where(is_lo, t, 0.0).astype(BF16)
        k_ref[0, 2 * j + 1] = jnp.where(is_lo, pltpu.roll(t, HALF, 1), 0.0).astype(BF16)

    ones = _ones_row((LANES, ts))
    for g in range(A_KV_HEADS):
        blk = (vt[g * LANES:(g + 1) * LANES, :] + ones).astype(BF16)
        for cc in range(ts // BLOCK):
            vt_ref[0, g, cc] = blk[:, cc * BLOCK:(cc + 1) * BLOCK]


def _proj_a(x, g, w_qt, w_k, w_vt, gq, gk, row_tabs, lane_tabs):
    B, S, _ = x.shape
    ts = min(PROJ_TILE, S)
    rtab = pl.BlockSpec((A_ROT_DIM // 2, ts), lambda b, s: (0, s))
    ltab = pl.BlockSpec((ts, LANES), lambda b, s: (s, 0))
    consts = [g, w_qt, w_k, w_vt, gq, gk]
    return pl.pallas_call(
        _proj_a_kernel,
        grid=(B, S // ts),
        in_specs=[pl.BlockSpec((1, ts, D_MODEL), lambda b, s: (b, s, 0))]
                 + [_resident(a.shape) for a in consts] + [rtab, rtab, ltab, ltab, ltab],
        out_specs=[pl.BlockSpec((1, A_HEADS, ts // BLOCK, LANES, BLOCK),
                                lambda b, s: (b, 0, s, 0, 0)),
                   pl.BlockSpec((1, A_KV_HEADS, ts, LANES), lambda b, s: (b, 0, s, 0)),
                   pl.BlockSpec((1, A_KV_HEADS, ts // BLOCK, LANES, BLOCK),
                                lambda b, s: (b, 0, s, 0, 0))],
        out_shape=[jax.ShapeDtypeStruct((B, A_HEADS, S // BLOCK, LANES, BLOCK), BF16),
                   jax.ShapeDtypeStruct((B, A_KV_HEADS, S, LANES), BF16),
                   jax.ShapeDtypeStruct((B, A_KV_HEADS, S // BLOCK, LANES, BLOCK), BF16)],
        compiler_params=_compiler_params(("parallel", "parallel")),
        name="proj_a",
    )(x, *consts, *row_tabs, *lane_tabs)


def _attn_a_kernel(sink_ref, qt_ref, k_ref, vt_ref, o_ref, s0_ref, s1_ref, *, blocks_per_step):
    step = pl.program_id(2)
    nb = vt_ref.shape[2]
    group = A_HEADS // A_KV_HEADS
    width = group * BLOCK
    key = lax.broadcasted_iota(jnp.int32, (BLOCK, width), 0)
    qry = lax.broadcasted_iota(jnp.int32, (BLOCK, width), 1) & (BLOCK - 1)
    sink = sink_ref[0] * LOG2E

    def neighbours(r):
        n = step * blocks_per_step + r
        return jnp.maximum(n - 1, 0), n, jnp.minimum(n + 1, nb - 1)

    def scores(r, s_ref):
        i_prev, n, i_next = neighbours(r)
        q4t = jnp.concatenate([qt_ref[0, hd, r] for hd in range(group)], axis=1)

        def block_scores(i):
            k = k_ref[0, 0, pl.ds(pl.multiple_of(i * BLOCK, BLOCK), BLOCK), :]
            return jnp.dot(k, q4t, preferred_element_type=F32)

        s_p = jnp.where((key >= qry) & (n > 0), block_scores(i_prev), MASKED)
        s_c = block_scores(n)
        s_n = jnp.where((key <= qry) & (n < nb - 1), block_scores(i_next), MASKED)
        s_ref[0], s_ref[1], s_ref[2] = s_p, s_c, s_n
        m = jnp.max(jnp.maximum(jnp.maximum(s_p, s_c), s_n), axis=0, keepdims=True)
        return jnp.maximum(m, sink)

    def finish(r, s_ref, m):
        acc = jnp.zeros((LANES, width), F32)
        for x, i in enumerate(neighbours(r)):
            acc += jnp.dot(vt_ref[0, 0, i], jnp.exp2(s_ref[x] - m).astype(BF16),
                           preferred_element_type=F32)
        denom = acc[B_V_DIM:B_V_DIM + 1] + jnp.exp2(sink - m)
        o = acc[:B_V_DIM] / denom
        rows = pl.ds(pl.multiple_of(r * BLOCK, BLOCK), BLOCK)
        for pi in range(group // 2):
            pair = jnp.concatenate([o[:, (2 * pi) * BLOCK:(2 * pi + 1) * BLOCK],
                                    o[:, (2 * pi + 1) * BLOCK:(2 * pi + 2) * BLOCK]], axis=0)
            o_ref[0, rows, pi * LANES:(pi + 1) * LANES] = pair.T.astype(BF16)

    def pair_of_blocks(i, m0, last):
        r = 2 * i
        m1 = scores(r + 1, s1_ref)
        finish(r, s0_ref, m0)
        m0 = None if last else scores(r + 2, s0_ref)
        finish(r + 1, s1_ref, m1)
        return m0

    n_pairs = blocks_per_step // 2
    m0 = lax.fori_loop(0, n_pairs - 1, lambda i, m: pair_of_blocks(i, m, False),
                       scores(0, s0_ref))
    pair_of_blocks(n_pairs - 1, m0, True)


def _attn_a(sink_rows, qt, k, vt):
    B, _, nb, _, _ = qt.shape
    S = nb * BLOCK
    r = min(ATTN_A_BLOCKS_PER_STEP, nb)
    assert r % 2 == 0 and nb % r == 0
    group = A_HEADS // A_KV_HEADS
    score_buf = pltpu.VMEM((3, BLOCK, group * BLOCK), F32)
    return pl.pallas_call(
        functools.partial(_attn_a_kernel, blocks_per_step=r),
        grid=(B, A_KV_HEADS, nb // r),
        in_specs=[pl.BlockSpec((1, 1, group * BLOCK), lambda b, g, n: (g, 0, 0)),
                  pl.BlockSpec((1, group, r, LANES, BLOCK), lambda b, g, n: (b, g, n, 0, 0)),
                  pl.BlockSpec((1, 1, S, LANES), lambda b, g, n: (b, g, 0, 0)),
                  pl.BlockSpec((1, 1, nb, LANES, BLOCK), lambda b, g, n: (b, g, 0, 0, 0))],
        out_specs=pl.BlockSpec((1, r * BLOCK, group * A_HEAD_DIM), lambda b, g, n: (b, n, g)),
        out_shape=jax.ShapeDtypeStruct((B, S, A_HEADS * A_HEAD_DIM), BF16),
        scratch_shapes=[score_buf, score_buf],
        compiler_params=_compiler_params(("parallel", "parallel", "arbitrary")),
        name="attn_a",
    )(sink_rows, qt, k, vt)


def _proj_b_kernel(x_ref, g_ref, win_ref, gql_ref, gkvl_ref, wuqt_ref, wuk_ref, wuvt_ref,
                   gq_ref, gk_ref, gksw_ref, ct_ref, st_ref, c_ref, ssg_ref,
                   q_ref, k_ref, vt_ref):
    x = x_ref[0]
    ts = x.shape[0]
    h = _rms(x, g_ref[...]).astype(BF16)
    lat = jnp.dot(h, win_ref[...], preferred_element_type=F32)
    cq = _rms(lat[:, :B_Q_RANK], gql_ref[...]).astype(BF16)
    ckv = _rms(lat[:, B_Q_RANK:B_Q_RANK + B_KV_RANK], gkvl_ref[...]).astype(BF16)
    rope_off = B_Q_RANK + B_KV_RANK
    k_rope = lat[:, rope_off:rope_off + LANES]
    k_rope_sw = lat[:, rope_off + LANES:]
    q_t = lax.dot_general(wuqt_ref[...], cq, NT_DIMS, preferred_element_type=F32)
    kn_all = jnp.dot(ckv, wuk_ref[...], preferred_element_type=F32)
    vt_all = lax.dot_general(wuvt_ref[...], ckv, NT_DIMS, preferred_element_type=F32)

    gq = _across_lanes(gq_ref[...], ts)
    cos_t, sin_t = ct_ref[...], st_ref[...]
    half = B_ROPE_DIM // 2
    gain_cos = c_ref[...] * gk_ref[...]
    rope_term = k_rope_sw * (ssg_ref[...] * gksw_ref[...])
    ones = _ones_row((MLA_VT_ROWS, ts))
    tk = vt_ref.shape[4]
    for hd in range(B_HEADS):
        sl_h = slice(hd * LANES, (hd + 1) * LANES)
        t = q_t[sl_h, :]
        ms = jnp.sum(t * t, axis=0, keepdims=True) / B_QK_DIM
        tn = t * lax.rsqrt(ms + RMS_EPS) * gq
        o1, o2 = _rope_rows(tn[B_NOPE_DIM:B_NOPE_DIM + half], tn[B_NOPE_DIM + half:B_QK_DIM],
                            cos_t, sin_t)
        q_ref[0, hd] = jnp.concatenate([tn[:B_NOPE_DIM], o1, o2, tn[B_QK_DIM:]],
                                       axis=0).astype(BF16)
        xk = kn_all[:, sl_h] + k_rope
        msk = jnp.sum(xk * xk, axis=-1, keepdims=True) / B_QK_DIM
        k_ref[0, hd] = ((xk * gain_cos + rope_term) * lax.rsqrt(msk + RMS_EPS)).astype(BF16)
        blk = (vt_all[hd * MLA_VT_ROWS:(hd + 1) * MLA_VT_ROWS, :] + ones).astype(BF16)
        for cc in range(ts // tk):
            vt_ref[0, hd, cc] = blk[:, cc * tk:(cc + 1) * tk]


def _proj_b(x, consts, row_tabs, lane_tabs):
    B, S, _ = x.shape
    ts = min(PROJ_TILE, S)
    tk = min(MLA_KV_TILE, max(LANES, S // 8))
    rtab = pl.BlockSpec((B_ROPE_DIM // 2, ts), lambda b, s: (0, s))
    ltab = pl.BlockSpec((ts, LANES), lambda b, s: (s, 0))
    return pl.pallas_call(
        _proj_b_kernel,
        grid=(B, S // ts),
        in_specs=[pl.BlockSpec((1, ts, D_MODEL), lambda b, s: (b, s, 0))]
                 + [_resident(a.shape) for a in consts] + [rtab, rtab, ltab, ltab],
        out_specs=[pl.BlockSpec((1, B_HEADS, LANES, ts), lambda b, s: (b, 0, 0, s)),
                   pl.BlockSpec((1, B_HEADS, ts, LANES), lambda b, s: (b, 0, s, 0)),
                   pl.BlockSpec((1, B_HEADS, ts // tk, MLA_VT_ROWS, tk),
                                lambda b, s: (b, 0, s, 0, 0))],
        out_shape=[jax.ShapeDtypeStruct((B, B_HEADS, LANES, S), BF16),
                   jax.ShapeDtypeStruct((B, B_HEADS, S, LANES), BF16),
                   jax.ShapeDtypeStruct((B, B_HEADS, S // tk, MLA_VT_ROWS, tk), BF16)],
        compiler_params=_compiler_params(("parallel", "parallel")),
        name="proj_b",
    )(x, *consts, *row_tabs, *lane_tabs)


def _attn_b_kernel(qt_ref, k_ref, vt_ref, o_ref, s0_ref, s1_ref, m_ref, acc_ref, *,
                   window_chunks):
    n_chunks, tk = vt_ref.shape[2], vt_ref.shape[4]
    tq = acc_ref.shape[2]
    for tile in range(qt_ref.shape[3] // tq):
        _attn_b_tile(qt_ref, k_ref, vt_ref, o_ref, s0_ref, s1_ref, m_ref, acc_ref,
                     slice(tile * tq, (tile + 1) * tq), n_chunks, tk, window_chunks)


def _attn_b_tile(qt_ref, k_ref, vt_ref, o_ref, s0_ref, s1_ref, m_ref, acc_ref, cols, n_chunks,
                 tk, window_chunks):
    m_ref[...] = jnp.full(m_ref.shape, MASKED, F32)
    acc_ref[...] = jnp.zeros(acc_ref.shape, F32)

    def scores(c, s_ref):
        start = pl.multiple_of(c * tk, tk)
        col_max = []
        for hh in range(2):
            s = jnp.dot(k_ref[0, hh, pl.ds(start, tk), :], qt_ref[0, hh, :, cols],
                        preferred_element_type=F32)
            s_ref[hh] = s
            col_max.append(jnp.max(s, axis=0, keepdims=True))
        return tuple(col_max)

    def accumulate(c, s_ref, col_max):
        for hh in range(2):
            m_old = m_ref[hh]
            m_new = jnp.maximum(m_old, col_max[hh])
            p = jnp.exp2(s_ref[hh] - m_new).astype(BF16)
            acc_ref[hh] = (acc_ref[hh] * jnp.exp2(m_old - m_new)
                           + jnp.dot(vt_ref[0, hh, c], p, preferred_element_type=F32))
            m_ref[hh] = m_new

    bufs = (s0_ref, s1_ref)

    def window(c0, col_max, last):
        for w in range(window_chunks):
            is_final = last and w == window_chunks - 1
            nxt = None if is_final else scores(c0 + w + 1, bufs[(w + 1) % 2])
            accumulate(c0 + w, bufs[w % 2], col_max)
            col_max = nxt
        return col_max

    n_windows = n_chunks // window_chunks
    col_max = lax.fori_loop(0, n_windows - 1,
                            lambda i, mx: window(i * window_chunks, mx, False),
                            scores(0, s0_ref))
    window((n_windows - 1) * window_chunks, col_max, True)
    halves = [acc_ref[hh][:B_V_DIM] / acc_ref[hh][B_V_DIM:B_V_DIM + 1] for hh in range(2)]
    o_ref[0, cols, :] = jnp.concatenate(halves, axis=0).T.astype(BF16)


def _attn_b(qt, k, vt):
    B, H, _, S = qt.shape
    tq = min(MLA_Q_TILE, S)
    tstep = min(MLA_TILES_PER_STEP * tq, S)
    n_chunks, tk = vt.shape[2], vt.shape[4]
    window = min(MLA_WINDOW_CHUNKS, n_chunks)
    assert n_chunks % window == 0 and window % 2 == 0
    score_buf = pltpu.VMEM((2, tk, tq), F32)
    return pl.pallas_call(
        functools.partial(_attn_b_kernel, window_chunks=window),
        grid=(B, H // 2, S // tstep),
        in_specs=[pl.BlockSpec((1, 2, LANES, tstep), lambda b, j, i: (b, j, 0, i)),
                  pl.BlockSpec((1, 2, S, LANES), lambda b, j, i: (b, j, 0, 0)),
                  pl.BlockSpec((1, 2, n_chunks, MLA_VT_ROWS, tk), lambda b, j, i: (b, j, 0, 0, 0))],
        out_specs=pl.BlockSpec((1, tstep, LANES), lambda b, j, i: (b, i, j)),
        out_shape=jax.ShapeDtypeStruct((B, S, H * B_V_DIM), BF16),
        scratch_shapes=[score_buf, score_buf, pltpu.VMEM((2, 1, tq), F32),
                        pltpu.VMEM((2, MLA_VT_ROWS, tq), F32)],
        compiler_params=_compiler_params(("parallel", "parallel", "arbitrary")),
        name="attn_b",
    )(qt, k, vt)


def _post_kernel(x_ref, o_ref, p_ref, wo_ref, fg_ref, wgu_ref, wd_ref, pg_ref, wpg_ref,
                 wpp_ref, out_ref):
    x = x_ref[...] + jnp.dot(o_ref[...], wo_ref[...], preferred_element_type=F32)
    h = _rms(x, fg_ref[...]).astype(BF16)
    gu = jnp.dot(h, wgu_ref[...], preferred_element_type=F32)
    gate, up = gu[:, :FFN_HIDDEN], gu[:, FFN_HIDDEN:]
    act = (gate * _sigmoid(gate) * up).astype(BF16)
    x = x + jnp.dot(act, wd_ref[...], preferred_element_type=F32)
    h = _rms(x, pg_ref[...]).astype(BF16)
    ple_gate = _sigmoid(jnp.dot(h, wpg_ref[...], preferred_element_type=F32))
    emb = jnp.dot(p_ref[0].astype(BF16), wpp_ref[...], preferred_element_type=F32)
    out_ref[...] = x + ple_gate * emb


def _post(x, o, p, layer, wo, fg, wgu, wd, pg, wpg, wpp):
    T = x.shape[0]
    tm = min(POST_TILE, T)
    consts = [wo, fg, wgu, wd, pg, wpg, wpp]
    row = lambda width: pl.BlockSpec((tm, width), lambda t: (t, 0))
    return pl.pallas_call(
        _post_kernel,
        grid=(T // tm,),
        in_specs=[row(D_MODEL), row(o.shape[1]),
                  pl.BlockSpec((1, tm, PLE_DIM), lambda t: (layer, t, 0))]
                 + [_resident(a.shape) for a in consts],
        out_specs=row(D_MODEL),
        out_shape=jax.ShapeDtypeStruct((T, D_MODEL), F32),
        compiler_params=_compiler_params(("parallel",)),
        name="post",
    )(x, o, p, *consts)


def _head_slots(w, heads, width):
    kdim = w.shape[0]
    w = w.reshape(kdim, heads, width)
    return jnp.pad(w, ((0, 0), (0, 0), (0, LANES - width))).reshape(kdim, heads * LANES)


def _lane_replicated(v):
    return jnp.broadcast_to(v[:, None], (v.shape[0], LANES))


def _prep_a(w_qkv, q_gain, k_gain, sink):
    scale = (A_HEAD_DIM ** -0.5) * LOG2E
    n_q = A_HEADS * A_HEAD_DIM
    n_qk = n_q + A_KV_HEADS * A_HEAD_DIM
    w_qt = w_qkv[:, :n_q].T.astype(BF16)
    w_k = w_qkv[:, n_q:n_qk].astype(BF16)
    w_vt = _head_slots(w_qkv[:, n_qk:], A_KV_HEADS, A_HEAD_DIM).T.astype(BF16)
    gq = _lane_replicated(q_gain * scale)
    gk = jnp.tile(k_gain, 2)[None, :]
    group = A_HEADS // A_KV_HEADS
    sink_rows = jnp.repeat(sink.reshape(A_KV_HEADS, group), BLOCK, axis=1)[:, None, :]
    return (w_qt, w_k, w_vt, gq, gk), sink_rows


def _prep_b(w_in, w_uq, w_ukv, q_gain, k_gain, q_lat_gain, kv_lat_gain, g):
    scale = (B_QK_DIM ** -0.5) * LOG2E
    half = B_ROPE_DIM // 2
    lat_cols = B_Q_RANK + B_KV_RANK
    rope_cols = w_in[:, lat_cols:]
    rope_sw_cols = jnp.concatenate([rope_cols[:, half:], rope_cols[:, :half]], axis=1)
    slot = lambda cols: jnp.pad(cols, ((0, 0), (B_NOPE_DIM, LANES - B_QK_DIM)))
    w_in_p = jnp.concatenate([w_in[:, :lat_cols], slot(rope_cols), slot(rope_sw_cols)],
                             axis=1).astype(BF16)
    w_uqt = _head_slots(w_uq, B_HEADS, B_QK_DIM).T.astype(BF16)
    ukv = w_ukv.reshape(B_KV_RANK, B_HEADS, B_NOPE_DIM + B_V_DIM)
    w_uk = _head_slots(ukv[:, :, :B_NOPE_DIM].reshape(B_KV_RANK, -1), B_HEADS, B_NOPE_DIM).astype(BF16)
    w_uv = jnp.pad(ukv[:, :, B_NOPE_DIM:], ((0, 0), (0, 0), (0, MLA_VT_ROWS - B_V_DIM)))
    w_uvt = w_uv.reshape(B_KV_RANK, -1).T.astype(BF16)
    pad = (0, LANES - B_QK_DIM)
    gq = _lane_replicated(jnp.pad(q_gain * scale, pad))
    gk = jnp.pad(k_gain, pad)[None, :]
    k_gain_sw = jnp.concatenate([k_gain[:B_NOPE_DIM], k_gain[B_NOPE_DIM + half:],
                                 k_gain[B_NOPE_DIM:B_NOPE_DIM + half]])
    gk_sw = jnp.pad(k_gain_sw, pad)[None, :]
    return [g, w_in_p, q_lat_gain[None, :], kv_lat_gain[None, :], w_uqt, w_uk, w_uvt,
            gq, gk, gk_sw]


def _trunk(x, p, mix_norm, layers_a, layers_b, a_w_o, b_w_o, ffn_norm, w_gu, w_d,
           ple_norm, w_pg, w_pp):
    B, S, _ = x.shape
    depth = p.shape[0]
    T = B * S
    p = p.reshape(depth, T, PLE_DIM)
    rows_a = _row_tables(S, A_ROT_DIM)
    lanes_a = _lane_tables(S, A_ROT_DIM, (0, HALF))
    rows_b = _row_tables(S, B_ROPE_DIM)
    c_b, lo_b, hi_b = _lane_tables(S, B_ROPE_DIM, (B_NOPE_DIM,))
    lanes_b = (c_b, lo_b + hi_b)
    for i in range(depth):
        j = i // 2
        g = mix_norm[i][None, :]
        if i % 2 == 0:
            consts, sink_rows = layers_a[j]
            qt, k, vt = _proj_a(x, g, *consts, rows_a, lanes_a)
            o = _attn_a(sink_rows, qt, k, vt)
            w_o = a_w_o[j]
        else:
            qt, k, vt = _proj_b(x, layers_b[j](g), rows_b, lanes_b)
            o = _attn_b(qt, k, vt)
            w_o = b_w_o[j]
        x = _post(x.reshape(T, D_MODEL), o.reshape(T, -1), p, i, w_o, ffn_norm[i][None, :],
                  w_gu[i], w_d[i], ple_norm[i][None, :], w_pg[i], w_pp[i]).reshape(B, S, D_MODEL)
    return x


def kernel(x_prompt, x_sample, p_prompt, p_sample, mix_norm, a_w_qkv, a_q_norm, a_k_norm, a_sink, a_w_o, b_w_in, b_q_lat_norm, b_kv_lat_norm, b_w_uq, b_w_ukv, b_q_norm, b_k_norm, b_w_o, ffn_norm, ffn_w_gate_up, ffn_w_down, ple_norm, ple_w_gate, ple_w_proj):
    layers_a = [_prep_a(a_w_qkv[j], a_q_norm[j], a_k_norm[j], a_sink[j])
                for j in range(a_w_qkv.shape[0])]
    layers_b = [functools.partial(_prep_b, b_w_in[j], b_w_uq[j], b_w_ukv[j], b_q_norm[j],
                                  b_k_norm[j], b_q_lat_norm[j], b_kv_lat_norm[j])
                for j in range(b_w_in.shape[0])]
    shared = (mix_norm, layers_a, layers_b, a_w_o.astype(BF16), b_w_o.astype(BF16),
              ffn_norm, ffn_w_gate_up.astype(BF16), ffn_w_down.astype(BF16), ple_norm,
              ple_w_gate.astype(BF16), ple_w_proj.astype(BF16))
    return (_trunk(x_prompt, p_prompt, *shared), _trunk(x_sample, p_sample, *shared))
```

```python
import functools
import math

import jax
import jax.numpy as jnp
from jax import lax
from jax.experimental import pallas as pl
from jax.experimental.pallas import tpu as pltpu

F32 = jnp.float32
BF16 = jnp.bfloat16

D_MODEL = 1024
PLE_DIM = 256
RMS_EPS = 1e-6
ROPE_THETA = 500000.0
BLOCK = 128
A_HEADS = 16
A_KV_HEADS = 4
A_HEAD_DIM = 64
A_ROT_DIM = 16
B_HEADS = 16
B_Q_RANK = 384
B_KV_RANK = 128
B_NOPE_DIM = 64
B_ROPE_DIM = 32
B_V_DIM = 64
B_QK_DIM = B_NOPE_DIM + B_ROPE_DIM
FFN_HIDDEN = 2816

LANES = 128
HALF = LANES // 2
VMEM_LIMIT_BYTES = 56 * 1024 * 1024

LOG2E = math.log2(math.e)
MASKED = -1e30
NT_DIMS = (((1,), (1,)), ((), ()))

PROJ_TILE = 512
POST_TILE = 512
MLA_HEADS_PER_STEP = 2
MLA_Q_TILE = 512
MLA_TILES_PER_STEP = 4
MLA_KV_TILE = 512
MLA_WINDOW_CHUNKS = 4
MLA_VT_ROWS = 80
ATTN_A_BLOCKS_PER_STEP = 64
ATTN_A_WINDOW_BLOCKS = 4


def _compiler_params(semantics):
    return pltpu.CompilerParams(dimension_semantics=semantics,
                                vmem_limit_bytes=VMEM_LIMIT_BYTES)


def _resident(shape):
    nd = len(shape)
    return pl.BlockSpec(shape, lambda *_: (0,) * nd, pipeline_mode=pl.Buffered(1))


def _rms(x, g):
    ms = jnp.mean(x * x, axis=-1, keepdims=True)
    return x * lax.rsqrt(ms + RMS_EPS) * g


def _sigmoid(x):
    return 1.0 / (1.0 + jnp.exp(-x))


def _lane_iota(shape):
    return lax.broadcasted_iota(jnp.int32, shape, len(shape) - 1)


def _across_lanes(col_block, n):
    return jnp.concatenate([col_block] * (n // LANES), axis=1)


def _rope_rows(x1, x2, cos_t, sin_t):
    return x1 * cos_t - x2 * sin_t, x1 * sin_t + x2 * cos_t


def _rope_lanes(x, c, s_from_below, s_from_above, shift):
    return (x * c + pltpu.roll(x, shift, 1) * s_from_below
            + pltpu.roll(x, LANES - shift, 1) * s_from_above)


def _angles(seq, rot_dim):
    inv = 1.0 / (ROPE_THETA ** (jnp.arange(0, rot_dim, 2, dtype=F32) / rot_dim))
    ang = jnp.arange(seq, dtype=F32)[:, None] * inv[None, :]
    return jnp.cos(ang), jnp.sin(ang)


def _lane_tables(seq, rot_dim, lane_offsets):
    half = rot_dim // 2
    cos, sin = _angles(seq, rot_dim)

    def place(block, off):
        return jnp.pad(block, ((0, 0), (off, LANES - off - half)))

    c = jnp.zeros((seq, LANES), F32)
    rotated = jnp.zeros((seq, LANES), F32)
    lo = jnp.zeros((seq, LANES), F32)
    hi = jnp.zeros((seq, LANES), F32)
    for off in lane_offsets:
        c = c + place(cos, off) + place(cos, off + half)
        rotated = rotated + place(jnp.ones_like(cos), off) + place(jnp.ones_like(cos), off + half)
        hi = hi + place(-sin, off)
        lo = lo + place(sin, off + half)
    return c + (1.0 - rotated), lo, hi


def _row_tables(seq, rot_dim):
    cos, sin = _angles(seq, rot_dim)
    return cos.T, sin.T


def _ones_row(shape):
    return (lax.broadcasted_iota(jnp.int32, shape, 0) == B_V_DIM).astype(F32)


def _proj_a_kernel(x_ref, g_ref, wqt_ref, wk_ref, wvt_ref, gq_ref, gk_ref, ct_ref, st_ref,
                   c_ref, sl_ref, sh_ref, q_ref, k_ref, vt_ref):
    x = x_ref[0]
    ts = x.shape[0]
    h = _rms(x, g_ref[...]).astype(BF16)
    q_t = lax.dot_general(wqt_ref[...], h, NT_DIMS, preferred_element_type=F32)
    kk = jnp.dot(h, wk_ref[...], preferred_element_type=F32)
    vt = lax.dot_general(wvt_ref[...], h, NT_DIMS, preferred_element_type=F32)

    gq = _across_lanes(gq_ref[...], ts)
    cos_t, sin_t = ct_ref[...], st_ref[...]
    half = A_ROT_DIM // 2
    pad = jnp.zeros((LANES - A_HEAD_DIM, BLOCK), BF16)
    for hd in range(A_HEADS):
        t = q_t[hd * A_HEAD_DIM:(hd + 1) * A_HEAD_DIM, :]
        ms = jnp.sum(t * t, axis=0, keepdims=True) / A_HEAD_DIM
        tn = t * lax.rsqrt(ms + RMS_EPS) * gq
        o1, o2 = _rope_rows(tn[:half], tn[half:A_ROT_DIM], cos_t, sin_t)
        out = jnp.concatenate([o1, o2, tn[A_ROT_DIM:]], axis=0).astype(BF16)
        for cc in range(ts // BLOCK):
            q_ref[0, hd, cc, :A_HEAD_DIM, :] = out[:, cc * BLOCK:(cc + 1) * BLOCK]
            q_ref[0, hd, cc, A_HEAD_DIM:, :] = pad

    c, sl, sh = c_ref[...], sl_ref[...], sh_ref[...]
    is_lo = _lane_iota((ts, LANES)) < HALF
    for j in range(A_KV_HEADS * A_HEAD_DIM // LANES):
        t = kk[:, j * LANES:(j + 1) * LANES]
        sq = t * t
        ss_lo = jnp.sum(jnp.where(is_lo, sq, 0.0), axis=-1, keepdims=True)
        ss_hi = jnp.sum(jnp.where(is_lo, 0.0, sq), axis=-1, keepdims=True)
        r = jnp.where(is_lo, lax.rsqrt(ss_lo / A_HEAD_DIM + RMS_EPS),
                      lax.rsqrt(ss_hi / A_HEAD_DIM + RMS_EPS))
        t = _rope_lanes(t * r * gk_ref[...], c, sl, sh, half)
        k_ref[0, 2 * j] = jnp.where(is_lo, t, 0.0).astype(BF16)
        k_ref[0, 2 * j + 1] = jnp.where(is_lo, pltpu.roll(t, HALF, 1), 0.0).astype(BF16)

    ones = _ones_row((LANES, ts))
    for g in range(A_KV_HEADS):
        blk = (vt[g * LANES:(g + 1) * LANES, :] + ones).astype(BF16)
        for cc in range(ts // BLOCK):
            vt_ref[0, g, cc] = blk[:, cc * BLOCK:(cc + 1) * BLOCK]


def _proj_a(x, g, w_qt, w_k, w_vt, gq, gk, row_tabs, lane_tabs):
    B, S, _ = x.shape
    ts = min(PROJ_TILE, S)
    rtab = pl.BlockSpec((A_ROT_DIM // 2, ts), lambda b, s: (0, s))
    ltab = pl.BlockSpec((ts, LANES), lambda b, s: (s, 0))
    consts = [g, w_qt, w_k, w_vt, gq, gk]
    return pl.pallas_call(
        _proj_a_kernel,
        grid=(B, S // ts),
        in_specs=[pl.BlockSpec((1, ts, D_MODEL), lambda b, s: (b, s, 0))]
                 + [_resident(a.shape) for a in consts] + [rtab, rtab, ltab, ltab, ltab],
        out_specs=[pl.BlockSpec((1, A_HEADS, ts // BLOCK, LANES, BLOCK),
                                lambda b, s: (b, 0, s, 0, 0)),
                   pl.BlockSpec((1, A_KV_HEADS, ts, LANES), lambda b, s: (b, 0, s, 0)),
                   pl.BlockSpec((1, A_KV_HEADS, ts // BLOCK, LANES, BLOCK),
                                lambda b, s: (b, 0, s, 0, 0))],
        out_shape=[jax.ShapeDtypeStruct((B, A_HEADS, S // BLOCK, LANES, BLOCK), BF16),
                   jax.ShapeDtypeStruct((B, A_KV_HEADS, S, LANES), BF16),
                   jax.ShapeDtypeStruct((B, A_KV_HEADS, S // BLOCK, LANES, BLOCK), BF16)],
        compiler_params=_compiler_params(("parallel", "parallel")),
        name="proj_a",
    )(x, *consts, *row_tabs, *lane_tabs)


def _attn_a_kernel(sink_ref, qt_ref, k_ref, vt_ref, o_ref, s0_ref, s1_ref, *, blocks_per_step):
    step = pl.program_id(2)
    nb = vt_ref.shape[2]
    group = A_HEADS // A_KV_HEADS
    width = group * BLOCK
    key = lax.broadcasted_iota(jnp.int32, (BLOCK, width), 0)
    qry = lax.broadcasted_iota(jnp.int32, (BLOCK, width), 1) & (BLOCK - 1)
    sink = sink_ref[0] * LOG2E

    def neighbours(r):
        n = step * blocks_per_step + r
        return jnp.maximum(n - 1, 0), n, jnp.minimum(n + 1, nb - 1)

    def scores(r, s_ref):
        i_prev, n, i_next = neighbours(r)
        q4t = jnp.concatenate([qt_ref[0, hd, r] for hd in range(group)], axis=1)

        def block_scores(i):
            k = k_ref[0, 0, pl.ds(pl.multiple_of(i * BLOCK, BLOCK), BLOCK), :]
            return jnp.dot(k, q4t, preferred_element_type=F32)

        s_p = jnp.where((key >= qry) & (n > 0), block_scores(i_prev), MASKED)
        s_c = block_scores(n)
        s_n = jnp.where((key <= qry) & (n < nb - 1), block_scores(i_next), MASKED)
        s_ref[0], s_ref[1], s_ref[2] = s_p, s_c, s_n
        m = jnp.max(jnp.maximum(jnp.maximum(s_p, s_c), s_n), axis=0, keepdims=True)
        return jnp.maximum(m, sink)

    def finish(r, s_ref, m):
        acc = jnp.zeros((LANES, width), F32)
        for x, i in enumerate(neighbours(r)):
            acc += jnp.dot(vt_ref[0, 0, i], jnp.exp2(s_ref[x] - m).astype(BF16),
                           preferred_element_type=F32)
        denom = acc[B_V_DIM:B_V_DIM + 1] + jnp.exp2(sink - m)
        o = acc[:B_V_DIM] / denom
        rows = pl.ds(pl.multiple_of(r * BLOCK, BLOCK), BLOCK)
        for pi in range(group // 2):
            pair = jnp.concatenate([o[:, (2 * pi) * BLOCK:(2 * pi + 1) * BLOCK],
                                    o[:, (2 * pi + 1) * BLOCK:(2 * pi + 2) * BLOCK]], axis=0)
            o_ref[0, rows, pi * LANES:(pi + 1) * LANES] = pair.T.astype(BF16)

    bufs = (s0_ref, s1_ref)
    window_blocks = min(ATTN_A_WINDOW_BLOCKS, blocks_per_step)

    def window(r0, m, last):
        for w in range(window_blocks):
            is_final = last and w == window_blocks - 1
            nxt = None if is_final else scores(r0 + w + 1, bufs[(w + 1) % 2])
            finish(r0 + w, bufs[w % 2], m)
            m = nxt
        return m

    n_windows = blocks_per_step // window_blocks
    m = lax.fori_loop(0, n_windows - 1, lambda i, m: window(i * window_blocks, m, False),
                      scores(0, s0_ref))
    window((n_windows - 1) * window_blocks, m, True)


def _attn_a(sink_rows, qt, k, vt):
    B, _, nb, _, _ = qt.shape
    S = nb * BLOCK
    r = min(ATTN_A_BLOCKS_PER_STEP, nb)
    assert r % 2 == 0 and nb % r == 0
    group = A_HEADS // A_KV_HEADS
    score_buf = pltpu.VMEM((3, BLOCK, group * BLOCK), F32)
    return pl.pallas_call(
        functools.partial(_attn_a_kernel, blocks_per_step=r),
        grid=(B, A_KV_HEADS, nb // r),
        in_specs=[pl.BlockSpec((1, 1, group * BLOCK), lambda b, g, n: (g, 0, 0)),
                  pl.BlockSpec((1, group, r, LANES, BLOCK), lambda b, g, n: (b, g, n, 0, 0)),
                  pl.BlockSpec((1, 1, S, LANES), lambda b, g, n: (b, g, 0, 0)),
                  pl.BlockSpec((1, 1, nb, LANES, BLOCK), lambda b, g, n: (b, g, 0, 0, 0))],
        out_specs=pl.BlockSpec((1, r * BLOCK, group * A_HEAD_DIM), lambda b, g, n: (b, n, g)),
        out_shape=jax.ShapeDtypeStruct((B, S, A_HEADS * A_HEAD_DIM), BF16),
        scratch_shapes=[score_buf, score_buf],
        compiler_params=_compiler_params(("parallel", "parallel", "arbitrary")),
        name="attn_a",
    )(sink_rows, qt, k, vt)


def _proj_b_kernel(x_ref, g_ref, win_ref, gql_ref, gkvl_ref, wuqt_ref, wuk_ref, wuvt_ref,
                   gq_ref, gk_ref, gksw_ref, ct_ref, st_ref, c_ref, ssg_ref,
                   q_ref, k_ref, vt_ref):
    x = x_ref[0]
    ts = x.shape[0]
    h = _rms(x, g_ref[...]).astype(BF16)
    lat = jnp.dot(h, win_ref[...], preferred_element_type=F32)
    cq = _rms(lat[:, :B_Q_RANK], gql_ref[...]).astype(BF16)
    ckv = _rms(lat[:, B_Q_RANK:B_Q_RANK + B_KV_RANK], gkvl_ref[...]).astype(BF16)
    rope_off = B_Q_RANK + B_KV_RANK
    k_rope = lat[:, rope_off:rope_off + LANES]
    k_rope_sw = lat[:, rope_off + LANES:]
    q_t = lax.dot_general(wuqt_ref[...], cq, NT_DIMS, preferred_element_type=F32)
    kn_all = jnp.dot(ckv, wuk_ref[...], preferred_element_type=F32)
    vt_all = lax.dot_general(wuvt_ref[...], ckv, NT_DIMS, preferred_element_type=F32)

    gq = _across_lanes(gq_ref[...], ts)
    cos_t, sin_t = ct_ref[...], st_ref[...]
    half = B_ROPE_DIM // 2
    gain_cos = c_ref[...] * gk_ref[...]
    rope_term = k_rope_sw * (ssg_ref[...] * gksw_ref[...])
    ones = _ones_row((MLA_VT_ROWS, ts))
    tk = vt_ref.shape[4]
    for hd in range(B_HEADS):
        sl_h = slice(hd * LANES, (hd + 1) * LANES)
        t = q_t[sl_h, :]
        ms = jnp.sum(t * t, axis=0, keepdims=True) / B_QK_DIM
        tn = t * lax.rsqrt(ms + RMS_EPS) * gq
        o1, o2 = _rope_rows(tn[B_NOPE_DIM:B_NOPE_DIM + half], tn[B_NOPE_DIM + half:B_QK_DIM],
                            cos_t, sin_t)
        q_ref[0, hd] = jnp.concatenate([tn[:B_NOPE_DIM], o1, o2, tn[B_QK_DIM:]],
                                       axis=0).astype(BF16)
        xk = kn_all[:, sl_h] + k_rope
        msk = jnp.sum(xk * xk, axis=-1, keepdims=True) / B_QK_DIM
        k_ref[0, hd] = ((xk * gain_cos + rope_term) * lax.rsqrt(msk + RMS_EPS)).astype(BF16)
        blk = (vt_all[hd * MLA_VT_ROWS:(hd + 1) * MLA_VT_ROWS, :] + ones).astype(BF16)
        for cc in range(ts // tk):
            vt_ref[0, hd, cc] = blk[:, cc * tk:(cc + 1) * tk]


def _proj_b(x, consts, row_tabs, lane_tabs):
    B, S, _ = x.shape
    ts = min(PROJ_TILE, S)
    tk = min(MLA_KV_TILE, max(LANES, S // 8))
    rtab = pl.BlockSpec((B_ROPE_DIM // 2, ts), lambda b, s: (0, s))
    ltab = pl.BlockSpec((ts, LANES), lambda b, s: (s, 0))
    return pl.pallas_call(
        _proj_b_kernel,
        grid=(B, S // ts),
        in_specs=[pl.BlockSpec((1, ts, D_MODEL), lambda b, s: (b, s, 0))]
                 + [_resident(a.shape) for a in consts] + [rtab, rtab, ltab, ltab],
        out_specs=[pl.BlockSpec((1, B_HEADS, LANES, ts), lambda b, s: (b, 0, 0, s)),
                   pl.BlockSpec((1, B_HEADS, ts, LANES), lambda b, s: (b, 0, s, 0)),
                   pl.BlockSpec((1, B_HEADS, ts // tk, MLA_VT_ROWS, tk),
                                lambda b, s: (b, 0, s, 0, 0))],
        out_shape=[jax.ShapeDtypeStruct((B, B_HEADS, LANES, S), BF16),
                   jax.ShapeDtypeStruct((B, B_HEADS, S, LANES), BF16),
                   jax.ShapeDtypeStruct((B, B_HEADS, S // tk, MLA_VT_ROWS, tk), BF16)],
        compiler_params=_compiler_params(("parallel", "parallel")),
        name="proj_b",
    )(x, *consts, *row_tabs, *lane_tabs)


def _attn_b_kernel(qt_ref, k_ref, vt_ref, o_ref, s0_ref, s1_ref, m_ref, acc_ref, *,
                   window_chunks):
    n_chunks, tk = vt_ref.shape[2], vt_ref.shape[4]
    tq = acc_ref.shape[2]
    for tile in range(qt_ref.shape[3] // tq):
        _attn_b_tile(qt_ref, k_ref, vt_ref, o_ref, s0_ref, s1_ref, m_ref, acc_ref,
                     slice(tile * tq, (tile + 1) * tq), n_chunks, tk, window_chunks)


def _attn_b_tile(qt_ref, k_ref, vt_ref, o_ref, s0_ref, s1_ref, m_ref, acc_ref, cols, n_chunks,
                 tk, window_chunks):
    m_ref[...] = jnp.full(m_ref.shape, MASKED, F32)
    acc_ref[...] = jnp.zeros(acc_ref.shape, F32)

    def scores(c, s_ref):
        start = pl.multiple_of(c * tk, tk)
        col_max = []
        for hh in range(m_ref.shape[0]):
            s = jnp.dot(k_ref[0, hh, pl.ds(start, tk), :], qt_ref[0, hh, :, cols],
                        preferred_element_type=F32)
            s_ref[hh] = s
            col_max.append(jnp.max(s, axis=0, keepdims=True))
        return tuple(col_max)

    def accumulate(c, s_ref, col_max):
        for hh in range(m_ref.shape[0]):
            m_old = m_ref[hh]
            m_new = jnp.maximum(m_old, col_max[hh])
            p = jnp.exp2(s_ref[hh] - m_new).astype(BF16)
            acc_ref[hh] = (acc_ref[hh] * jnp.exp2(m_old - m_new)
                           + jnp.dot(vt_ref[0, hh, c], p, preferred_element_type=F32))
            m_ref[hh] = m_new

    bufs = (s0_ref, s1_ref)

    def window(c0, col_max, last):
        for w in range(window_chunks):
            is_final = last and w == window_chunks - 1
            nxt = None if is_final else scores(c0 + w + 1, bufs[(w + 1) % 2])
            accumulate(c0 + w, bufs[w % 2], col_max)
            col_max = nxt
        return col_max

    n_windows = n_chunks // window_chunks
    col_max = lax.fori_loop(0, n_windows - 1,
                            lambda i, mx: window(i * window_chunks, mx, False),
                            scores(0, s0_ref))
    window((n_windows - 1) * window_chunks, col_max, True)
    heads = [acc_ref[hh][:B_V_DIM] / acc_ref[hh][B_V_DIM:B_V_DIM + 1]
             for hh in range(m_ref.shape[0])]
    for pi in range(len(heads) // 2):
        pair = jnp.concatenate(heads[2 * pi:2 * pi + 2], axis=0)
        o_ref[0, cols, pi * LANES:(pi + 1) * LANES] = pair.T.astype(BF16)


def _attn_b(qt, k, vt):
    B, H, _, S = qt.shape
    tq = min(MLA_Q_TILE, S)
    tstep = min(MLA_TILES_PER_STEP * tq, S)
    n_chunks, tk = vt.shape[2], vt.shape[4]
    window = min(MLA_WINDOW_CHUNKS, n_chunks)
    assert n_chunks % window == 0 and window % 2 == 0
    hs = MLA_HEADS_PER_STEP
    score_buf = pltpu.VMEM((hs, tk, tq), F32)
    return pl.pallas_call(
        functools.partial(_attn_b_kernel, window_chunks=window),
        grid=(B, H // hs, S // tstep),
        in_specs=[pl.BlockSpec((1, hs, LANES, tstep), lambda b, j, i: (b, j, 0, i)),
                  pl.BlockSpec((1, hs, S, LANES), lambda b, j, i: (b, j, 0, 0)),
                  pl.BlockSpec((1, hs, n_chunks, MLA_VT_ROWS, tk),
                               lambda b, j, i: (b, j, 0, 0, 0))],
        out_specs=pl.BlockSpec((1, tstep, hs * B_V_DIM), lambda b, j, i: (b, i, j)),
        out_shape=jax.ShapeDtypeStruct((B, S, H * B_V_DIM), BF16),
        scratch_shapes=[score_buf, score_buf, pltpu.VMEM((hs, 1, tq), F32),
                        pltpu.VMEM((hs, MLA_VT_ROWS, tq), F32)],
        compiler_params=_compiler_params(("parallel", "parallel", "arbitrary")),
        name="attn_b",
    )(qt, k, vt)


def _post_kernel(x_ref, o_ref, p_ref, wo_ref, fg_ref, wgu_ref, wd_ref, pg_ref, wpg_ref,
                 wpp_ref, out_ref):
    x = x_ref[...] + jnp.dot(o_ref[...], wo_ref[...], preferred_element_type=F32)
    h = _rms(x, fg_ref[...]).astype(BF16)
    gu = jnp.dot(h, wgu_ref[...], preferred_element_type=F32)
    gate, up = gu[:, :FFN_HIDDEN], gu[:, FFN_HIDDEN:]
    act = (gate * _sigmoid(gate) * up).astype(BF16)
    x = x + jnp.dot(act, wd_ref[...], preferred_element_type=F32)
    h = _rms(x, pg_ref[...]).astype(BF16)
    ple_gate = _sigmoid(jnp.dot(h, wpg_ref[...], preferred_element_type=F32))
    emb = jnp.dot(p_ref[0].astype(BF16), wpp_ref[...], preferred_element_type=F32)
    out_ref[...] = x + ple_gate * emb


def _post(x, o, p, layer, wo, fg, wgu, wd, pg, wpg, wpp):
    T = x.shape[0]
    tm = min(POST_TILE, T)
    consts = [wo, fg, wgu, wd, pg, wpg, wpp]
    row = lambda width: pl.BlockSpec((tm, width), lambda t: (t, 0))
    return pl.pallas_call(
        _post_kernel,
        grid=(T // tm,),
        in_specs=[row(D_MODEL), row(o.shape[1]),
                  pl.BlockSpec((1, tm, PLE_DIM), lambda t: (layer, t, 0))]
                 + [_resident(a.shape) for a in consts],
        out_specs=row(D_MODEL),
        out_shape=jax.ShapeDtypeStruct((T, D_MODEL), F32),
        compiler_params=_compiler_params(("parallel",)),
        name="post",
    )(x, o, p, *consts)


def _head_slots(w, heads, width):
    kdim = w.shape[0]
    w = w.reshape(kdim, heads, width)
    return jnp.pad(w, ((0, 0), (0, 0), (0, LANES - width))).reshape(kdim, heads * LANES)


def _lane_replicated(v):
    return jnp.broadcast_to(v[:, None], (v.shape[0], LANES))


def _prep_a(w_qkv, q_gain, k_gain, sink):
    scale = (A_HEAD_DIM ** -0.5) * LOG2E
    n_q = A_HEADS * A_HEAD_DIM
    n_qk = n_q + A_KV_HEADS * A_HEAD_DIM
    w_qt = w_qkv[:, :n_q].T.astype(BF16)
    w_k = w_qkv[:, n_q:n_qk].astype(BF16)
    w_vt = _head_slots(w_qkv[:, n_qk:], A_KV_HEADS, A_HEAD_DIM).T.astype(BF16)
    gq = _lane_replicated(q_gain * scale)
    gk = jnp.tile(k_gain, 2)[None, :]
    group = A_HEADS // A_KV_HEADS
    sink_rows = jnp.repeat(sink.reshape(A_KV_HEADS, group), BLOCK, axis=1)[:, None, :]
    return (w_qt, w_k, w_vt, gq, gk), sink_rows


def _prep_b(w_in, w_uq, w_ukv, q_gain, k_gain, q_lat_gain, kv_lat_gain, g):
    scale = (B_QK_DIM ** -0.5) * LOG2E
    half = B_ROPE_DIM // 2
    lat_cols = B_Q_RANK + B_KV_RANK
    rope_cols = w_in[:, lat_cols:]
    rope_sw_cols = jnp.concatenate([rope_cols[:, half:], rope_cols[:, :half]], axis=1)
    slot = lambda cols: jnp.pad(cols, ((0, 0), (B_NOPE_DIM, LANES - B_QK_DIM)))
    w_in_p = jnp.concatenate([w_in[:, :lat_cols], slot(rope_cols), slot(rope_sw_cols)],
                             axis=1).astype(BF16)
    w_uqt = _head_slots(w_uq, B_HEADS, B_QK_DIM).T.astype(BF16)
    ukv = w_ukv.reshape(B_KV_RANK, B_HEADS, B_NOPE_DIM + B_V_DIM)
    w_uk = _head_slots(ukv[:, :, :B_NOPE_DIM].reshape(B_KV_RANK, -1), B_HEADS, B_NOPE_DIM).astype(BF16)
    w_uv = jnp.pad(ukv[:, :, B_NOPE_DIM:], ((0, 0), (0, 0), (0, MLA_VT_ROWS - B_V_DIM)))
    w_uvt = w_uv.reshape(B_KV_RANK, -1).T.astype(BF16)
    pad = (0, LANES - B_QK_DIM)
    gq = _lane_replicated(jnp.pad(q_gain * scale, pad))
    gk = jnp.pad(k_gain, pad)[None, :]
    k_gain_sw = jnp.concatenate([k_gain[:B_NOPE_DIM], k_gain[B_NOPE_DIM + half:],
                                 k_gain[B_NOPE_DIM:B_NOPE_DIM + half]])
    gk_sw = jnp.pad(k_gain_sw, pad)[None, :]
    return [g, w_in_p, q_lat_gain[None, :], kv_lat_gain[None, :], w_uqt, w_uk, w_uvt,
            gq, gk, gk_sw]


def _trunk(x, p, mix_norm, layers_a, layers_b, a_w_o, b_w_o, ffn_norm, w_gu, w_d,
           ple_norm, w_pg, w_pp):
    B, S, _ = x.shape
    depth = p.shape[0]
    T = B * S
    p = p.reshape(depth, T, PLE_DIM)
    rows_a = _row_tables(S, A_ROT_DIM)
    lanes_a = _lane_tables(S, A_ROT_DIM, (0, HALF))
    rows_b = _row_tables(S, B_ROPE_DIM)
    c_b, lo_b, hi_b = _lane_tables(S, B_ROPE_DIM, (B_NOPE_DIM,))
    lanes_b = (c_b, lo_b + hi_b)
    for i in range(depth):
        j = i // 2
        g = mix_norm[i][None, :]
        if i % 2 == 0:
            consts, sink_rows = layers_a[j]
            qt, k, vt = _proj_a(x, g, *consts, rows_a, lanes_a)
            o = _attn_a(sink_rows, qt, k, vt)
            w_o = a_w_o[j]
        else:
            qt, k, vt = _proj_b(x, layers_b[j](g), rows_b, lanes_b)
            o = _attn_b(qt, k, vt)
            w_o = b_w_o[j]
        x = _post(x.reshape(T, D_MODEL), o.reshape(T, -1), p, i, w_o, ffn_norm[i][None, :],
                  w_gu[i], w_d[i], ple_norm[i][None, :], w_pg[i], w_pp[i]).reshape(B, S, D_MODEL)
    return x


def kernel(x_prompt, x_sample, p_prompt, p_sample, mix_norm, a_w_qkv, a_q_norm, a_k_norm, a_sink, a_w_o, b_w_in, b_q_lat_norm, b_kv_lat_norm, b_w_uq, b_w_ukv, b_q_norm, b_k_norm, b_w_o, ffn_norm, ffn_w_gate_up, ffn_w_down, ple_norm, ple_w_gate, ple_w_proj):
    layers_a = [_prep_a(a_w_qkv[j], a_q_norm[j], a_k_norm[j], a_sink[j])
                for j in range(a_w_qkv.shape[0])]
    layers_b = [functools.partial(_prep_b, b_w_in[j], b_w_uq[j], b_w_ukv[j], b_q_norm[j],
                                  b_k_norm[j], b_q_lat_norm[j], b_kv_lat_norm[j])
                for j in range(b_w_in.shape[0])]
    shared = (mix_norm, layers_a, layers_b, a_w_o.astype(BF16), b_w_o.astype(BF16),
              ffn_norm, ffn_w_gate_up.astype(BF16), ffn_w_down.astype(BF16), ple_norm,
              ple_w_gate.astype(BF16), ple_w_proj.astype(BF16))
    return (_trunk(x_prompt, p_prompt, *shared), _trunk(x_sample, p_sample, *shared))
```

```python
import functools
import math

import jax
import jax.numpy as jnp
from jax import lax
from jax.experimental import pallas as pl
from jax.experimental.pallas import tpu as pltpu

F32 = jnp.float32
BF16 = jnp.bfloat16

D_MODEL = 1024
PLE_DIM = 256
RMS_EPS = 1e-6
ROPE_THETA = 500000.0
BLOCK = 128
A_HEADS = 16
A_KV_HEADS = 4
A_HEAD_DIM = 64
A_ROT_DIM = 16
B_HEADS = 16
B_Q_RANK = 384
B_KV_RANK = 128
B_NOPE_DIM = 64
B_ROPE_DIM = 32
B_V_DIM = 64
B_QK_DIM = B_NOPE_DIM + B_ROPE_DIM
FFN_HIDDEN = 2816

LANES = 128
HALF = LANES // 2
VMEM_LIMIT_BYTES = 56 * 1024 * 1024

LOG2E = math.log2(math.e)
MASKED = -1e30
NT_DIMS = (((1,), (1,)), ((), ()))

PROJ_A_TILE = 512
PROJ_B_TILE = 1024
POST_TILE = 512
MLA_HEADS_PER_STEP = 2
MLA_Q_TILE = 512
MLA_TILES_PER_STEP = 4
MLA_KV_TILE = 512
MLA_WINDOW_CHUNKS = 4
MLA_VT_ROWS = 80
ATTN_A_BLOCKS_PER_STEP = 64
ATTN_A_WINDOW_BLOCKS = 8


def _compiler_params(semantics):
    return pltpu.CompilerParams(dimension_semantics=semantics,
                                vmem_limit_bytes=VMEM_LIMIT_BYTES)


def _resident(shape):
    nd = len(shape)
    return pl.BlockSpec(shape, lambda *_: (0,) * nd, pipeline_mode=pl.Buffered(1))


def _rms(x, g):
    ms = jnp.mean(x * x, axis=-1, keepdims=True)
    return x * lax.rsqrt(ms + RMS_EPS) * g


def _sigmoid(x):
    return 1.0 / (1.0 + jnp.exp(-x))


def _lane_iota(shape):
    return lax.broadcasted_iota(jnp.int32, shape, len(shape) - 1)


def _across_lanes(col_block, n):
    return jnp.concatenate([col_block] * (n // LANES), axis=1)


def _rope_rows(x1, x2, cos_t, sin_t):
    return x1 * cos_t - x2 * sin_t, x1 * sin_t + x2 * cos_t


def _rope_lanes(x, c, s_from_below, s_from_above, shift):
    return (x * c + pltpu.roll(x, shift, 1) * s_from_below
            + pltpu.roll(x, LANES - shift, 1) * s_from_above)


def _angles(seq, rot_dim):
    inv = 1.0 / (ROPE_THETA ** (jnp.arange(0, rot_dim, 2, dtype=F32) / rot_dim))
    ang = jnp.arange(seq, dtype=F32)[:, None] * inv[None, :]
    return jnp.cos(ang), jnp.sin(ang)


def _lane_tables(seq, rot_dim, lane_offsets):
    half = rot_dim // 2
    cos, sin = _angles(seq, rot_dim)

    def place(block, off):
        return jnp.pad(block, ((0, 0), (off, LANES - off - half)))

    c = jnp.zeros((seq, LANES), F32)
    rotated = jnp.zeros((seq, LANES), F32)
    lo = jnp.zeros((seq, LANES), F32)
    hi = jnp.zeros((seq, LANES), F32)
    for off in lane_offsets:
        c = c + place(cos, off) + place(cos, off + half)
        rotated = rotated + place(jnp.ones_like(cos), off) + place(jnp.ones_like(cos), off + half)
        hi = hi + place(-sin, off)
        lo = lo + place(sin, off + half)
    return c + (1.0 - rotated), lo, hi


def _row_tables(seq, rot_dim):
    cos, sin = _angles(seq, rot_dim)
    return cos.T, sin.T


def _ones_row(shape):
    return (lax.broadcasted_iota(jnp.int32, shape, 0) == B_V_DIM).astype(F32)


def _proj_a_kernel(x_ref, g_ref, wqt_ref, wk_ref, wvt_ref, gq_ref, gk_ref, ct_ref, st_ref,
                   c_ref, sl_ref, sh_ref, q_ref, k_ref, vt_ref):
    x = x_ref[0]
    ts = x.shape[0]
    h = _rms(x, g_ref[...]).astype(BF16)
    q_t = lax.dot_general(wqt_ref[...], h, NT_DIMS, preferred_element_type=F32)
    kk = jnp.dot(h, wk_ref[...], preferred_element_type=F32)
    vt = lax.dot_general(wvt_ref[...], h, NT_DIMS, preferred_element_type=F32)

    gq = _across_lanes(gq_ref[...], ts)
    cos_t, sin_t = ct_ref[...], st_ref[...]
    half = A_ROT_DIM // 2
    pad = jnp.zeros((LANES - A_HEAD_DIM, BLOCK), BF16)
    for hd in range(A_HEADS):
        t = q_t[hd * A_HEAD_DIM:(hd + 1) * A_HEAD_DIM, :]
        ms = jnp.sum(t * t, axis=0, keepdims=True) / A_HEAD_DIM
        tn = t * lax.rsqrt(ms + RMS_EPS) * gq
        o1, o2 = _rope_rows(tn[:half], tn[half:A_ROT_DIM], cos_t, sin_t)
        out = jnp.concatenate([o1, o2, tn[A_ROT_DIM:]], axis=0).astype(BF16)
        for cc in range(ts // BLOCK):
            q_ref[0, hd, cc, :A_HEAD_DIM, :] = out[:, cc * BLOCK:(cc + 1) * BLOCK]
            q_ref[0, hd, cc, A_HEAD_DIM:, :] = pad

    c, sl, sh = c_ref[...], sl_ref[...], sh_ref[...]
    is_lo = _lane_iota((ts, LANES)) < HALF
    for j in range(A_KV_HEADS * A_HEAD_DIM // LANES):
        t = kk[:, j * LANES:(j + 1) * LANES]
        sq = t * t
        ss_lo = jnp.sum(jnp.where(is_lo, sq, 0.0), axis=-1, keepdims=True)
        ss_hi = jnp.sum(jnp.where(is_lo, 0.0, sq), axis=-1, keepdims=True)
        r = jnp.where(is_lo, lax.rsqrt(ss_lo / A_HEAD_DIM + RMS_EPS),
                      lax.rsqrt(ss_hi / A_HEAD_DIM + RMS_EPS))
        t = _rope_lanes(t * r * gk_ref[...], c, sl, sh, half)
        k_ref[0, 2 * j] = jnp.where(is_lo, t, 0.0).astype(BF16)
        k_ref[0, 2 * j + 1] = jnp.where(is_lo, pltpu.roll(t, HALF, 1), 0.0).astype(BF16)

    ones = _ones_row((LANES, ts))
    for g in range(A_KV_HEADS):
        blk = (vt[g * LANES:(g + 1) * LANES, :] + ones).astype(BF16)
        for cc in range(ts // BLOCK):
            vt_ref[0, g, cc] = blk[:, cc * BLOCK:(cc + 1) * BLOCK]


def _proj_a(x, g, w_qt, w_k, w_vt, gq, gk, row_tabs, lane_tabs):
    B, S, _ = x.shape
    ts = min(PROJ_A_TILE, S)
    rtab = pl.BlockSpec((A_ROT_DIM // 2, ts), lambda b, s: (0, s))
    ltab = pl.BlockSpec((ts, LANES), lambda b, s: (s, 0))
    consts = [g, w_qt, w_k, w_vt, gq, gk]
    return pl.pallas_call(
        _proj_a_kernel,
        grid=(B, S // ts),
        in_specs=[pl.BlockSpec((1, ts, D_MODEL), lambda b, s: (b, s, 0))]
                 + [_resident(a.shape) for a in consts] + [rtab, rtab, ltab, ltab, ltab],
        out_specs=[pl.BlockSpec((1, A_HEADS, ts // BLOCK, LANES, BLOCK),
                                lambda b, s: (b, 0, s, 0, 0)),
                   pl.BlockSpec((1, A_KV_HEADS, ts, LANES), lambda b, s: (b, 0, s, 0)),
                   pl.BlockSpec((1, A_KV_HEADS, ts // BLOCK, LANES, BLOCK),
                                lambda b, s: (b, 0, s, 0, 0))],
        out_shape=[jax.ShapeDtypeStruct((B, A_HEADS, S // BLOCK, LANES, BLOCK), BF16),
                   jax.ShapeDtypeStruct((B, A_KV_HEADS, S, LANES), BF16),
                   jax.ShapeDtypeStruct((B, A_KV_HEADS, S // BLOCK, LANES, BLOCK), BF16)],
        compiler_params=_compiler_params(("parallel", "parallel")),
        name="proj_a",
    )(x, *consts, *row_tabs, *lane_tabs)


def _attn_a_kernel(sink_ref, qt_ref, k_ref, vt_ref, o_ref, s0_ref, s1_ref, *, blocks_per_step):
    step = pl.program_id(2)
    nb = vt_ref.shape[2]
    group = A_HEADS // A_KV_HEADS
    width = group * BLOCK
    key = lax.broadcasted_iota(jnp.int32, (BLOCK, width), 0)
    qry = lax.broadcasted_iota(jnp.int32, (BLOCK, width), 1) & (BLOCK - 1)
    sink = sink_ref[0] * LOG2E

    def neighbours(r):
        n = step * blocks_per_step + r
        return jnp.maximum(n - 1, 0), n, jnp.minimum(n + 1, nb - 1)

    def scores(r, s_ref):
        i_prev, n, i_next = neighbours(r)
        q4t = jnp.concatenate([qt_ref[0, hd, r] for hd in range(group)], axis=1)

        def block_scores(i):
            k = k_ref[0, 0, pl.ds(pl.multiple_of(i * BLOCK, BLOCK), BLOCK), :]
            return jnp.dot(k, q4t, preferred_element_type=F32)

        s_p = jnp.where((key >= qry) & (n > 0), block_scores(i_prev), MASKED)
        s_c = block_scores(n)
        s_n = jnp.where((key <= qry) & (n < nb - 1), block_scores(i_next), MASKED)
        s_ref[0], s_ref[1], s_ref[2] = s_p, s_c, s_n
        m = jnp.max(jnp.maximum(jnp.maximum(s_p, s_c), s_n), axis=0, keepdims=True)
        return jnp.maximum(m, sink)

    def finish(r, s_ref, m):
        acc = jnp.zeros((LANES, width), F32)
        for x, i in enumerate(neighbours(r)):
            acc += jnp.dot(vt_ref[0, 0, i], jnp.exp2(s_ref[x] - m).astype(BF16),
                           preferred_element_type=F32)
        denom = acc[B_V_DIM:B_V_DIM + 1] + jnp.exp2(sink - m)
        o = acc[:B_V_DIM] / denom
        rows = pl.ds(pl.multiple_of(r * BLOCK, BLOCK), BLOCK)
        for pi in range(group // 2):
            pair = jnp.concatenate([o[:, (2 * pi) * BLOCK:(2 * pi + 1) * BLOCK],
                                    o[:, (2 * pi + 1) * BLOCK:(2 * pi + 2) * BLOCK]], axis=0)
            o_ref[0, rows, pi * LANES:(pi + 1) * LANES] = pair.T.astype(BF16)

    bufs = (s0_ref, s1_ref)
    window_blocks = min(ATTN_A_WINDOW_BLOCKS, blocks_per_step)

    def window(r0, m, last):
        for w in range(window_blocks):
            is_final = last and w == window_blocks - 1
            nxt = None if is_final else scores(r0 + w + 1, bufs[(w + 1) % 2])
            finish(r0 + w, bufs[w % 2], m)
            m = nxt
        return m

    n_windows = blocks_per_step // window_blocks
    m = lax.fori_loop(0, n_windows - 1, lambda i, m: window(i * window_blocks, m, False),
                      scores(0, s0_ref))
    window((n_windows - 1) * window_blocks, m, True)


def _attn_a(sink_rows, qt, k, vt):
    B, _, nb, _, _ = qt.shape
    S = nb * BLOCK
    r = min(ATTN_A_BLOCKS_PER_STEP, nb)
    assert r % 2 == 0 and nb % r == 0
    group = A_HEADS // A_KV_HEADS
    score_buf = pltpu.VMEM((3, BLOCK, group * BLOCK), F32)
    return pl.pallas_call(
        functools.partial(_attn_a_kernel, blocks_per_step=r),
        grid=(B, A_KV_HEADS, nb // r),
        in_specs=[pl.BlockSpec((1, 1, group * BLOCK), lambda b, g, n: (g, 0, 0)),
                  pl.BlockSpec((1, group, r, LANES, BLOCK), lambda b, g, n: (b, g, n, 0, 0)),
                  pl.BlockSpec((1, 1, S, LANES), lambda b, g, n: (b, g, 0, 0)),
                  pl.BlockSpec((1, 1, nb, LANES, BLOCK), lambda b, g, n: (b, g, 0, 0, 0))],
        out_specs=pl.BlockSpec((1, r * BLOCK, group * A_HEAD_DIM), lambda b, g, n: (b, n, g)),
        out_shape=jax.ShapeDtypeStruct((B, S, A_HEADS * A_HEAD_DIM), BF16),
        scratch_shapes=[score_buf, score_buf],
        compiler_params=_compiler_params(("parallel", "parallel", "arbitrary")),
        name="attn_a",
    )(sink_rows, qt, k, vt)


def _proj_b_kernel(x_ref, g_ref, win_ref, gql_ref, gkvl_ref, wuqt_ref, wuk_ref, wuvt_ref,
                   gq_ref, gk_ref, gksw_ref, ct_ref, st_ref, c_ref, ssg_ref,
                   q_ref, k_ref, vt_ref):
    x = x_ref[0]
    ts = x.shape[0]
    h = _rms(x, g_ref[...]).astype(BF16)
    lat = jnp.dot(h, win_ref[...], preferred_element_type=F32)
    cq = _rms(lat[:, :B_Q_RANK], gql_ref[...]).astype(BF16)
    ckv = _rms(lat[:, B_Q_RANK:B_Q_RANK + B_KV_RANK], gkvl_ref[...]).astype(BF16)
    rope_off = B_Q_RANK + B_KV_RANK
    k_rope = lat[:, rope_off:rope_off + LANES]
    k_rope_sw = lat[:, rope_off + LANES:]
    q_t = lax.dot_general(wuqt_ref[...], cq, NT_DIMS, preferred_element_type=F32)
    kn_all = jnp.dot(ckv, wuk_ref[...], preferred_element_type=F32)
    vt_all = lax.dot_general(wuvt_ref[...], ckv, NT_DIMS, preferred_element_type=F32)

    gq = _across_lanes(gq_ref[...], ts)
    cos_t, sin_t = ct_ref[...], st_ref[...]
    half = B_ROPE_DIM // 2
    gain_cos = c_ref[...] * gk_ref[...]
    rope_term = k_rope_sw * (ssg_ref[...] * gksw_ref[...])
    ones = _ones_row((MLA_VT_ROWS, ts))
    tk = vt_ref.shape[4]
    for hd in range(B_HEADS):
        sl_h = slice(hd * LANES, (hd + 1) * LANES)
        t = q_t[sl_h, :]
        ms = jnp.sum(t * t, axis=0, keepdims=True) / B_QK_DIM
        tn = t * lax.rsqrt(ms + RMS_EPS) * gq
        o1, o2 = _rope_rows(tn[B_NOPE_DIM:B_NOPE_DIM + half], tn[B_NOPE_DIM + half:B_QK_DIM],
                            cos_t, sin_t)
        q_ref[0, hd] = jnp.concatenate([tn[:B_NOPE_DIM], o1, o2, tn[B_QK_DIM:]],
                                       axis=0).astype(BF16)
        xk = kn_all[:, sl_h] + k_rope
        msk = jnp.sum(xk * xk, axis=-1, keepdims=True) / B_QK_DIM
        k_ref[0, hd] = ((xk * gain_cos + rope_term) * lax.rsqrt(msk + RMS_EPS)).astype(BF16)
        blk = (vt_all[hd * MLA_VT_ROWS:(hd + 1) * MLA_VT_ROWS, :] + ones).astype(BF16)
        for cc in range(ts // tk):
            vt_ref[0, hd, cc] = blk[:, cc * tk:(cc + 1) * tk]


def _proj_b(x, consts, row_tabs, lane_tabs):
    B, S, _ = x.shape
    ts = min(PROJ_B_TILE, S)
    tk = min(MLA_KV_TILE, max(LANES, S // 8))
    rtab = pl.BlockSpec((B_ROPE_DIM // 2, ts), lambda b, s: (0, s))
    ltab = pl.BlockSpec((ts, LANES), lambda b, s: (s, 0))
    return pl.pallas_call(
        _proj_b_kernel,
        grid=(B, S // ts),
        in_specs=[pl.BlockSpec((1, ts, D_MODEL), lambda b, s: (b, s, 0))]
                 + [_resident(a.shape) for a in consts] + [rtab, rtab, ltab, ltab],
        out_specs=[pl.BlockSpec((1, B_HEADS, LANES, ts), lambda b, s: (b, 0, 0, s)),
                   pl.BlockSpec((1, B_HEADS, ts, LANES), lambda b, s: (b, 0, s, 0)),
                   pl.BlockSpec((1, B_HEADS, ts // tk, MLA_VT_ROWS, tk),
                                lambda b, s: (b, 0, s, 0, 0))],
        out_shape=[jax.ShapeDtypeStruct((B, B_HEADS, LANES, S), BF16),
                   jax.ShapeDtypeStruct((B, B_HEADS, S, LANES), BF16),
                   jax.ShapeDtypeStruct((B, B_HEADS, S // tk, MLA_VT_ROWS, tk), BF16)],
        compiler_params=_compiler_params(("parallel", "parallel")),
        name="proj_b",
    )(x, *consts, *row_tabs, *lane_tabs)


def _attn_b_kernel(qt_ref, k_ref, vt_ref, o_ref, s0_ref, s1_ref, m_ref, acc_ref, *,
                   window_chunks):
    n_chunks, tk = vt_ref.shape[2], vt_ref.shape[4]
    tq = acc_ref.shape[2]
    for tile in range(qt_ref.shape[3] // tq):
        _attn_b_tile(qt_ref, k_ref, vt_ref, o_ref, s0_ref, s1_ref, m_ref, acc_ref,
                     slice(tile * tq, (tile + 1) * tq), n_chunks, tk, window_chunks)


def _attn_b_tile(qt_ref, k_ref, vt_ref, o_ref, s0_ref, s1_ref, m_ref, acc_ref, cols, n_chunks,
                 tk, window_chunks):
    m_ref[...] = jnp.full(m_ref.shape, MASKED, F32)
    acc_ref[...] = jnp.zeros(acc_ref.shape, F32)

    def scores(c, s_ref):
        start = pl.multiple_of(c * tk, tk)
        col_max = []
        for hh in range(m_ref.shape[0]):
            s = jnp.dot(k_ref[0, hh, pl.ds(start, tk), :], qt_ref[0, hh, :, cols],
                        preferred_element_type=F32)
            s_ref[hh] = s
            col_max.append(jnp.max(s, axis=0, keepdims=True))
        return tuple(col_max)

    def accumulate(c, s_ref, col_max):
        for hh in range(m_ref.shape[0]):
            m_old = m_ref[hh]
            m_new = jnp.maximum(m_old, col_max[hh])
            p = jnp.exp2(s_ref[hh] - m_new).astype(BF16)
            acc_ref[hh] = (acc_ref[hh] * jnp.exp2(m_old - m_new)
                           + jnp.dot(vt_ref[0, hh, c], p, preferred_element_type=F32))
            m_ref[hh] = m_new

    bufs = (s0_ref, s1_ref)

    def window(c0, col_max, last):
        for w in range(window_chunks):
            is_final = last and w == window_chunks - 1
            nxt = None if is_final else scores(c0 + w + 1, bufs[(w + 1) % 2])
            accumulate(c0 + w, bufs[w % 2], col_max)
            col_max = nxt
        return col_max

    n_windows = n_chunks // window_chunks
    col_max = lax.fori_loop(0, n_windows - 1,
                            lambda i, mx: window(i * window_chunks, mx, False),
                            scores(0, s0_ref))
    window((n_windows - 1) * window_chunks, col_max, True)
    heads = [acc_ref[hh][:B_V_DIM] / acc_ref[hh][B_V_DIM:B_V_DIM + 1]
             for hh in range(m_ref.shape[0])]
    for pi in range(len(heads) // 2):
        pair = jnp.concatenate(heads[2 * pi:2 * pi + 2], axis=0)
        o_ref[0, cols, pi * LANES:(pi + 1) * LANES] = pair.T.astype(BF16)


def _attn_b(qt, k, vt):
    B, H, _, S = qt.shape
    tq = min(MLA_Q_TILE, S)
    tstep = min(MLA_TILES_PER_STEP * tq, S)
    n_chunks, tk = vt.shape[2], vt.shape[4]
    window = min(MLA_WINDOW_CHUNKS, n_chunks)
    assert n_chunks % window == 0 and window % 2 == 0
    hs = MLA_HEADS_PER_STEP
    score_buf = pltpu.VMEM((hs, tk, tq), F32)
    return pl.pallas_call(
        functools.partial(_attn_b_kernel, window_chunks=window),
        grid=(B, H // hs, S // tstep),
        in_specs=[pl.BlockSpec((1, hs, LANES, tstep), lambda b, j, i: (b, j, 0, i)),
                  pl.BlockSpec((1, hs, S, LANES), lambda b, j, i: (b, j, 0, 0)),
                  pl.BlockSpec((1, hs, n_chunks, MLA_VT_ROWS, tk),
                               lambda b, j, i: (b, j, 0, 0, 0))],
        out_specs=pl.BlockSpec((1, tstep, hs * B_V_DIM), lambda b, j, i: (b, i, j)),
        out_shape=jax.ShapeDtypeStruct((B, S, H * B_V_DIM), BF16),
        scratch_shapes=[score_buf, score_buf, pltpu.VMEM((hs, 1, tq), F32),
                        pltpu.VMEM((hs, MLA_VT_ROWS, tq), F32)],
        compiler_params=_compiler_params(("parallel", "parallel", "arbitrary")),
        name="attn_b",
    )(qt, k, vt)


def _post_kernel(x_ref, o_ref, p_ref, wo_ref, fg_ref, wgu_ref, wd_ref, pg_ref, wpg_ref,
                 wpp_ref, out_ref):
    x = x_ref[...] + jnp.dot(o_ref[...], wo_ref[...], preferred_element_type=F32)
    h = _rms(x, fg_ref[...]).astype(BF16)
    gu = jnp.dot(h, wgu_ref[...], preferred_element_type=F32)
    gate, up = gu[:, :FFN_HIDDEN], gu[:, FFN_HIDDEN:]
    act = (gate * _sigmoid(gate) * up).astype(BF16)
    x = x + jnp.dot(act, wd_ref[...], preferred_element_type=F32)
    h = _rms(x, pg_ref[...]).astype(BF16)
    ple_gate = _sigmoid(jnp.dot(h, wpg_ref[...], preferred_element_type=F32))
    emb = jnp.dot(p_ref[0].astype(BF16), wpp_ref[...], preferred_element_type=F32)
    out_ref[...] = x + ple_gate * emb


def _post(x, o, p, layer, wo, fg, wgu, wd, pg, wpg, wpp):
    T = x.shape[0]
    tm = min(POST_TILE, T)
    consts = [wo, fg, wgu, wd, pg, wpg, wpp]
    row = lambda width: pl.BlockSpec((tm, width), lambda t: (t, 0))
    return pl.pallas_call(
        _post_kernel,
        grid=(T // tm,),
        in_specs=[row(D_MODEL), row(o.shape[1]),
                  pl.BlockSpec((1, tm, PLE_DIM), lambda t: (layer, t, 0))]
                 + [_resident(a.shape) for a in consts],
        out_specs=row(D_MODEL),
        out_shape=jax.ShapeDtypeStruct((T, D_MODEL), F32),
        compiler_params=_compiler_params(("parallel",)),
        name="post",
    )(x, o, p, *consts)


def _head_slots(w, heads, width):
    kdim = w.shape[0]
    w = w.reshape(kdim, heads, width)
    return jnp.pad(w, ((0, 0), (0, 0), (0, LANES - width))).reshape(kdim, heads * LANES)


def _lane_replicated(v):
    return jnp.broadcast_to(v[:, None], (v.shape[0], LANES))


def _prep_a(w_qkv, q_gain, k_gain, sink):
    scale = (A_HEAD_DIM ** -0.5) * LOG2E
    n_q = A_HEADS * A_HEAD_DIM
    n_qk = n_q + A_KV_HEADS * A_HEAD_DIM
    w_qt = w_qkv[:, :n_q].T.astype(BF16)
    w_k = w_qkv[:, n_q:n_qk].astype(BF16)
    w_vt = _head_slots(w_qkv[:, n_qk:], A_KV_HEADS, A_HEAD_DIM).T.astype(BF16)
    gq = _lane_replicated(q_gain * scale)
    gk = jnp.tile(k_gain, 2)[None, :]
    group = A_HEADS // A_KV_HEADS
    sink_rows = jnp.repeat(sink.reshape(A_KV_HEADS, group), BLOCK, axis=1)[:, None, :]
    return (w_qt, w_k, w_vt, gq, gk), sink_rows


def _prep_b(w_in, w_uq, w_ukv, q_gain, k_gain, q_lat_gain, kv_lat_gain, g):
    scale = (B_QK_DIM ** -0.5) * LOG2E
    half = B_ROPE_DIM // 2
    lat_cols = B_Q_RANK + B_KV_RANK
    rope_cols = w_in[:, lat_cols:]
    rope_sw_cols = jnp.concatenate([rope_cols[:, half:], rope_cols[:, :half]], axis=1)
    slot = lambda cols: jnp.pad(cols, ((0, 0), (B_NOPE_DIM, LANES - B_QK_DIM)))
    w_in_p = jnp.concatenate([w_in[:, :lat_cols], slot(rope_cols), slot(rope_sw_cols)],
                             axis=1).astype(BF16)
    w_uqt = _head_slots(w_uq, B_HEADS, B_QK_DIM).T.astype(BF16)
    ukv = w_ukv.reshape(B_KV_RANK, B_HEADS, B_NOPE_DIM + B_V_DIM)
    w_uk = _head_slots(ukv[:, :, :B_NOPE_DIM].reshape(B_KV_RANK, -1), B_HEADS, B_NOPE_DIM).astype(BF16)
    w_uv = jnp.pad(ukv[:, :, B_NOPE_DIM:], ((0, 0), (0, 0), (0, MLA_VT_ROWS - B_V_DIM)))
    w_uvt = w_uv.reshape(B_KV_RANK, -1).T.astype(BF16)
    pad = (0, LANES - B_QK_DIM)
    gq = _lane_replicated(jnp.pad(q_gain * scale, pad))
    gk = jnp.pad(k_gain, pad)[None, :]
    k_gain_sw = jnp.concatenate([k_gain[:B_NOPE_DIM], k_gain[B_NOPE_DIM + half:],
                                 k_gain[B_NOPE_DIM:B_NOPE_DIM + half]])
    gk_sw = jnp.pad(k_gain_sw, pad)[None, :]
    return [g, w_in_p, q_lat_gain[None, :], kv_lat_gain[None, :], w_uqt, w_uk, w_uvt,
            gq, gk, gk_sw]


def _trunk(x, p, mix_norm, layers_a, layers_b, a_w_o, b_w_o, ffn_norm, w_gu, w_d,
           ple_norm, w_pg, w_pp):
    B, S, _ = x.shape
    depth = p.shape[0]
    T = B * S
    p = p.reshape(depth, T, PLE_DIM)
    rows_a = _row_tables(S, A_ROT_DIM)
    lanes_a = _lane_tables(S, A_ROT_DIM, (0, HALF))
    rows_b = _row_tables(S, B_ROPE_DIM)
    c_b, lo_b, hi_b = _lane_tables(S, B_ROPE_DIM, (B_NOPE_DIM,))
    lanes_b = (c_b, lo_b + hi_b)
    for i in range(depth):
        j = i // 2
        g = mix_norm[i][None, :]
        if i % 2 == 0:
            consts, sink_rows = layers_a[j]
            qt, k, vt = _proj_a(x, g, *consts, rows_a, lanes_a)
            o = _attn_a(sink_rows, qt, k, vt)
            w_o = a_w_o[j]
        else:
            qt, k, vt = _proj_b(x, layers_b[j](g), rows_b, lanes_b)
            o = _attn_b(qt, k, vt)
            w_o = b_w_o[j]
        x = _post(x.reshape(T, D_MODEL), o.reshape(T, -1), p, i, w_o, ffn_norm[i][None, :],
                  w_gu[i], w_d[i], ple_norm[i][None, :], w_pg[i], w_pp[i]).reshape(B, S, D_MODEL)
    return x


def kernel(x_prompt, x_sample, p_prompt, p_sample, mix_norm, a_w_qkv, a_q_norm, a_k_norm, a_sink, a_w_o, b_w_in, b_q_lat_norm, b_kv_lat_norm, b_w_uq, b_w_ukv, b_q_norm, b_k_norm, b_w_o, ffn_norm, ffn_w_gate_up, ffn_w_down, ple_norm, ple_w_gate, ple_w_proj):
    layers_a = [_prep_a(a_w_qkv[j], a_q_norm[j], a_k_norm[j], a_sink[j])
                for j in range(a_w_qkv.shape[0])]
    layers_b = [functools.partial(_prep_b, b_w_in[j], b_w_uq[j], b_w_ukv[j], b_q_norm[j],
                                  b_k_norm[j], b_q_lat_norm[j], b_kv_lat_norm[j])
                for j in range(b_w_in.shape[0])]
    shared = (mix_norm, layers_a, layers_b, a_w_o.astype(BF16), b_w_o.astype(BF16),
              ffn_norm, ffn_w_gate_up.astype(BF16), ffn_w_down.astype(BF16), ple_norm,
              ple_w_gate.astype(BF16), ple_w_proj.astype(BF16))
    return (_trunk(x_prompt, p_prompt, *shared), _trunk(x_sample, p_sample, *shared))
```

```python
import functools
import math

import jax
import jax.numpy as jnp
from jax import lax
from jax.experimental import pallas as pl
from jax.experimental.pallas import tpu as pltpu

F32 = jnp.float32
BF16 = jnp.bfloat16

D_MODEL = 1024
PLE_DIM = 256
RMS_EPS = 1e-6
ROPE_THETA = 500000.0
BLOCK = 128
A_HEADS = 16
A_KV_HEADS = 4
A_HEAD_DIM = 64
A_ROT_DIM = 16
B_HEADS = 16
B_Q_RANK = 384
B_KV_RANK = 128
B_NOPE_DIM = 64
B_ROPE_DIM = 32
B_V_DIM = 64
B_QK_DIM = B_NOPE_DIM + B_ROPE_DIM
FFN_HIDDEN = 2816

LANES = 128
HALF = LANES // 2
VMEM_LIMIT_BYTES = 56 * 1024 * 1024

LOG2E = math.log2(math.e)
MASKED = -1e30
NT_DIMS = (((1,), (1,)), ((), ()))

PROJ_A_TILE = 512
PROJ_B_TILE = 1024
POST_TILE = 512
MLA_HEADS_PER_STEP = 2
MLA_Q_TILE = 512
MLA_TILES_PER_STEP = 4
MLA_KV_TILE = 512
MLA_WINDOW_CHUNKS = 4
MLA_VT_ROWS = 80
ATTN_A_BLOCKS_PER_STEP = 64
ATTN_A_WINDOW_BLOCKS = 16


def _compiler_params(semantics):
    return pltpu.CompilerParams(dimension_semantics=semantics,
                                vmem_limit_bytes=VMEM_LIMIT_BYTES)


def _resident(shape):
    nd = len(shape)
    return pl.BlockSpec(shape, lambda *_: (0,) * nd, pipeline_mode=pl.Buffered(1))


def _rms(x, g):
    ms = jnp.mean(x * x, axis=-1, keepdims=True)
    return x * lax.rsqrt(ms + RMS_EPS) * g


def _sigmoid(x):
    return 1.0 / (1.0 + jnp.exp(-x))


def _lane_iota(shape):
    return lax.broadcasted_iota(jnp.int32, shape, len(shape) - 1)


def _across_lanes(col_block, n):
    return jnp.concatenate([col_block] * (n // LANES), axis=1)


def _rope_rows(x1, x2, cos_t, sin_t):
    return x1 * cos_t - x2 * sin_t, x1 * sin_t + x2 * cos_t


def _rope_lanes(x, c, s_from_below, s_from_above, shift):
    return (x * c + pltpu.roll(x, shift, 1) * s_from_below
            + pltpu.roll(x, LANES - shift, 1) * s_from_above)


def _angles(seq, rot_dim):
    inv = 1.0 / (ROPE_THETA ** (jnp.arange(0, rot_dim, 2, dtype=F32) / rot_dim))
    ang = jnp.arange(seq, dtype=F32)[:, None] * inv[None, :]
    return jnp.cos(ang), jnp.sin(ang)


def _lane_tables(seq, rot_dim, lane_offsets):
    half = rot_dim // 2
    cos, sin = _angles(seq, rot_dim)

    def place(block, off):
        return jnp.pad(block, ((0, 0), (off, LANES - off - half)))

    c = jnp.zeros((seq, LANES), F32)
    rotated = jnp.zeros((seq, LANES), F32)
    lo = jnp.zeros((seq, LANES), F32)
    hi = jnp.zeros((seq, LANES), F32)
    for off in lane_offsets:
        c = c + place(cos, off) + place(cos, off + half)
        rotated = rotated + place(jnp.ones_like(cos), off) + place(jnp.ones_like(cos), off + half)
        hi = hi + place(-sin, off)
        lo = lo + place(sin, off + half)
    return c + (1.0 - rotated), lo, hi


def _row_tables(seq, rot_dim):
    cos, sin = _angles(seq, rot_dim)
    return cos.T, sin.T


def _ones_row(shape):
    return (lax.broadcasted_iota(jnp.int32, shape, 0) == B_V_DIM).astype(F32)


def _proj_a_kernel(x_ref, g_ref, wqt_ref, wk_ref, wvt_ref, gq_ref, gk_ref, ct_ref, st_ref,
                   c_ref, sl_ref, sh_ref, q_ref, k_ref, vt_ref):
    x = x_ref[0]
    ts = x.shape[0]
    h = _rms(x, g_ref[...]).astype(BF16)
    q_t = lax.dot_general(wqt_ref[...], h, NT_DIMS, preferred_element_type=F32)
    kk = jnp.dot(h, wk_ref[...], preferred_element_type=F32)
    vt = lax.dot_general(wvt_ref[...], h, NT_DIMS, preferred_element_type=F32)

    gq = _across_lanes(gq_ref[...], ts)
    cos_t, sin_t = ct_ref[...], st_ref[...]
    half = A_ROT_DIM // 2
    pad = jnp.zeros((LANES - A_HEAD_DIM, BLOCK), BF16)
    for hd in range(A_HEADS):
        t = q_t[hd * A_HEAD_DIM:(hd + 1) * A_HEAD_DIM, :]
        ms = jnp.sum(t * t, axis=0, keepdims=True) / A_HEAD_DIM
        tn = t * lax.rsqrt(ms + RMS_EPS) * gq
        o1, o2 = _rope_rows(tn[:half], tn[half:A_ROT_DIM], cos_t, sin_t)
        out = jnp.concatenate([o1, o2, tn[A_ROT_DIM:]], axis=0).astype(BF16)
        for cc in range(ts // BLOCK):
            q_ref[0, hd, cc, :A_HEAD_DIM, :] = out[:, cc * BLOCK:(cc + 1) * BLOCK]
            q_ref[0, hd, cc, A_HEAD_DIM:, :] = pad

    c, sl, sh = c_ref[...], sl_ref[...], sh_ref[...]
    is_lo = _lane_iota((ts, LANES)) < HALF
    for j in range(A_KV_HEADS * A_HEAD_DIM // LANES):
        t = kk[:, j * LANES:(j + 1) * LANES]
        sq = t * t
        ss_lo = jnp.sum(jnp.where(is_lo, sq, 0.0), axis=-1, keepdims=True)
        ss_hi = jnp.sum(jnp.where(is_lo, 0.0, sq), axis=-1, keepdims=True)
        r = jnp.where(is_lo, lax.rsqrt(ss_lo / A_HEAD_DIM + RMS_EPS),
                      lax.rsqrt(ss_hi / A_HEAD_DIM + RMS_EPS))
        t = _rope_lanes(t * r * gk_ref[...], c, sl, sh, half)
        k_ref[0, 2 * j] = jnp.where(is_lo, t, 0.0).astype(BF16)
        k_ref[0, 2 * j + 1] = jnp.where(is_lo, pltpu.roll(t, HALF, 1), 0.0).astype(BF16)

    ones = _ones_row((LANES, ts))
    for g in range(A_KV_HEADS):
        blk = (vt[g * LANES:(g + 1) * LANES, :] + ones).astype(BF16)
        for cc in range(ts // BLOCK):
            vt_ref[0, g, cc] = blk[:, cc * BLOCK:(cc + 1) * BLOCK]


def _proj_a(x, g, w_qt, w_k, w_vt, gq, gk, row_tabs, lane_tabs):
    B, S, _ = x.shape
    ts = min(PROJ_A_TILE, S)
    rtab = pl.BlockSpec((A_ROT_DIM // 2, ts), lambda b, s: (0, s))
    ltab = pl.BlockSpec((ts, LANES), lambda b, s: (s, 0))
    consts = [g, w_qt, w_k, w_vt, gq, gk]
    return pl.pallas_call(
        _proj_a_kernel,
        grid=(B, S // ts),
        in_specs=[pl.BlockSpec((1, ts, D_MODEL), lambda b, s: (b, s, 0))]
                 + [_resident(a.shape) for a in consts] + [rtab, rtab, ltab, ltab, ltab],
        out_specs=[pl.BlockSpec((1, A_HEADS, ts // BLOCK, LANES, BLOCK),
                                lambda b, s: (b, 0, s, 0, 0)),
                   pl.BlockSpec((1, A_KV_HEADS, ts, LANES), lambda b, s: (b, 0, s, 0)),
                   pl.BlockSpec((1, A_KV_HEADS, ts // BLOCK, LANES, BLOCK),
                                lambda b, s: (b, 0, s, 0, 0))],
        out_shape=[jax.ShapeDtypeStruct((B, A_HEADS, S // BLOCK, LANES, BLOCK), BF16),
                   jax.ShapeDtypeStruct((B, A_KV_HEADS, S, LANES), BF16),
                   jax.ShapeDtypeStruct((B, A_KV_HEADS, S // BLOCK, LANES, BLOCK), BF16)],
        compiler_params=_compiler_params(("parallel", "parallel")),
        name="proj_a",
    )(x, *consts, *row_tabs, *lane_tabs)


def _attn_a_kernel(sink_ref, qt_ref, k_ref, vt_ref, o_ref, s0_ref, s1_ref, *, blocks_per_step):
    step = pl.program_id(2)
    nb = vt_ref.shape[2]
    group = A_HEADS // A_KV_HEADS
    width = group * BLOCK
    key = lax.broadcasted_iota(jnp.int32, (BLOCK, width), 0)
    qry = lax.broadcasted_iota(jnp.int32, (BLOCK, width), 1) & (BLOCK - 1)
    sink = sink_ref[0] * LOG2E

    def neighbours(r):
        n = step * blocks_per_step + r
        return jnp.maximum(n - 1, 0), n, jnp.minimum(n + 1, nb - 1)

    def scores(r, s_ref):
        i_prev, n, i_next = neighbours(r)
        q4t = jnp.concatenate([qt_ref[0, hd, r] for hd in range(group)], axis=1)

        def block_scores(i):
            k = k_ref[0, 0, pl.ds(pl.multiple_of(i * BLOCK, BLOCK), BLOCK), :]
            return jnp.dot(k, q4t, preferred_element_type=F32)

        s_p = jnp.where((key >= qry) & (n > 0), block_scores(i_prev), MASKED)
        s_c = block_scores(n)
        s_n = jnp.where((key <= qry) & (n < nb - 1), block_scores(i_next), MASKED)
        s_ref[0], s_ref[1], s_ref[2] = s_p, s_c, s_n
        m = jnp.max(jnp.maximum(jnp.maximum(s_p, s_c), s_n), axis=0, keepdims=True)
        return jnp.maximum(m, sink)

    def finish(r, s_ref, m):
        acc = jnp.zeros((LANES, width), F32)
        for x, i in enumerate(neighbours(r)):
            acc += jnp.dot(vt_ref[0, 0, i], jnp.exp2(s_ref[x] - m).astype(BF16),
                           preferred_element_type=F32)
        denom = acc[B_V_DIM:B_V_DIM + 1] + jnp.exp2(sink - m)
        o = acc[:B_V_DIM] / denom
        rows = pl.ds(pl.multiple_of(r * BLOCK, BLOCK), BLOCK)
        for pi in range(group // 2):
            pair = jnp.concatenate([o[:, (2 * pi) * BLOCK:(2 * pi + 1) * BLOCK],
                                    o[:, (2 * pi + 1) * BLOCK:(2 * pi + 2) * BLOCK]], axis=0)
            o_ref[0, rows, pi * LANES:(pi + 1) * LANES] = pair.T.astype(BF16)

    bufs = (s0_ref, s1_ref)
    window_blocks = min(ATTN_A_WINDOW_BLOCKS, blocks_per_step)

    def window(r0, m, last):
        for w in range(window_blocks):
            is_final = last and w == window_blocks - 1
            nxt = None if is_final else scores(r0 + w + 1, bufs[(w + 1) % 2])
            finish(r0 + w, bufs[w % 2], m)
            m = nxt
        return m

    n_windows = blocks_per_step // window_blocks
    m = lax.fori_loop(0, n_windows - 1, lambda i, m: window(i * window_blocks, m, False),
                      scores(0, s0_ref))
    window((n_windows - 1) * window_blocks, m, True)


def _attn_a(sink_rows, qt, k, vt):
    B, _, nb, _, _ = qt.shape
    S = nb * BLOCK
    r = min(ATTN_A_BLOCKS_PER_STEP, nb)
    assert r % 2 == 0 and nb % r == 0
    group = A_HEADS // A_KV_HEADS
    score_buf = pltpu.VMEM((3, BLOCK, group * BLOCK), F32)
    return pl.pallas_call(
        functools.partial(_attn_a_kernel, blocks_per_step=r),
        grid=(B, A_KV_HEADS, nb // r),
        in_specs=[pl.BlockSpec((1, 1, group * BLOCK), lambda b, g, n: (g, 0, 0)),
                  pl.BlockSpec((1, group, r, LANES, BLOCK), lambda b, g, n: (b, g, n, 0, 0)),
                  pl.BlockSpec((1, 1, S, LANES), lambda b, g, n: (b, g, 0, 0)),
                  pl.BlockSpec((1, 1, nb, LANES, BLOCK), lambda b, g, n: (b, g, 0, 0, 0))],
        out_specs=pl.BlockSpec((1, r * BLOCK, group * A_HEAD_DIM), lambda b, g, n: (b, n, g)),
        out_shape=jax.ShapeDtypeStruct((B, S, A_HEADS * A_HEAD_DIM), BF16),
        scratch_shapes=[score_buf, score_buf],
        compiler_params=_compiler_params(("parallel", "parallel", "arbitrary")),
        name="attn_a",
    )(sink_rows, qt, k, vt)


def _proj_b_kernel(x_ref, g_ref, win_ref, gql_ref, gkvl_ref, wuqt_ref, wuk_ref, wuvt_ref,
                   gq_ref, gk_ref, gksw_ref, ct_ref, st_ref, c_ref, ssg_ref,
                   q_ref, k_ref, vt_ref):
    x = x_ref[0]
    ts = x.shape[0]
    h = _rms(x, g_ref[...]).astype(BF16)
    lat = jnp.dot(h, win_ref[...], preferred_element_type=F32)
    cq = _rms(lat[:, :B_Q_RANK], gql_ref[...]).astype(BF16)
    ckv = _rms(lat[:, B_Q_RANK:B_Q_RANK + B_KV_RANK], gkvl_ref[...]).astype(BF16)
    rope_off = B_Q_RANK + B_KV_RANK
    k_rope = lat[:, rope_off:rope_off + LANES]
    k_rope_sw = lat[:, rope_off + LANES:]
    q_t = lax.dot_general(wuqt_ref[...], cq, NT_DIMS, preferred_element_type=F32)
    kn_all = jnp.dot(ckv, wuk_ref[...], preferred_element_type=F32)
    vt_all = lax.dot_general(wuvt_ref[...], ckv, NT_DIMS, preferred_element_type=F32)

    gq = _across_lanes(gq_ref[...], ts)
    cos_t, sin_t = ct_ref[...], st_ref[...]
    half = B_ROPE_DIM // 2
    gain_cos = c_ref[...] * gk_ref[...]
    rope_term = k_rope_sw * (ssg_ref[...] * gksw_ref[...])
    ones = _ones_row((MLA_VT_ROWS, ts))
    tk = vt_ref.shape[4]
    for hd in range(B_HEADS):
        sl_h = slice(hd * LANES, (hd + 1) * LANES)
        t = q_t[sl_h, :]
        ms = jnp.sum(t * t, axis=0, keepdims=True) / B_QK_DIM
        tn = t * lax.rsqrt(ms + RMS_EPS) * gq
        o1, o2 = _rope_rows(tn[B_NOPE_DIM:B_NOPE_DIM + half], tn[B_NOPE_DIM + half:B_QK_DIM],
                            cos_t, sin_t)
        q_ref[0, hd] = jnp.concatenate([tn[:B_NOPE_DIM], o1, o2, tn[B_QK_DIM:]],
                                       axis=0).astype(BF16)
        xk = kn_all[:, sl_h] + k_rope
        msk = jnp.sum(xk * xk, axis=-1, keepdims=True) / B_QK_DIM
        k_ref[0, hd] = ((xk * gain_cos + rope_term) * lax.rsqrt(msk + RMS_EPS)).astype(BF16)
        blk = (vt_all[hd * MLA_VT_ROWS:(hd + 1) * MLA_VT_ROWS, :] + ones).astype(BF16)
        for cc in range(ts // tk):
            vt_ref[0, hd, cc] = blk[:, cc * tk:(cc + 1) * tk]


def _proj_b(x, consts, row_tabs, lane_tabs):
    B, S, _ = x.shape
    ts = min(PROJ_B_TILE, S)
    tk = min(MLA_KV_TILE, max(LANES, S // 8))
    rtab = pl.BlockSpec((B_ROPE_DIM // 2, ts), lambda b, s: (0, s))
    ltab = pl.BlockSpec((ts, LANES), lambda b, s: (s, 0))
    return pl.pallas_call(
        _proj_b_kernel,
        grid=(B, S // ts),
        in_specs=[pl.BlockSpec((1, ts, D_MODEL), lambda b, s: (b, s, 0))]
                 + [_resident(a.shape) for a in consts] + [rtab, rtab, ltab, ltab],
        out_specs=[pl.BlockSpec((1, B_HEADS, LANES, ts), lambda b, s: (b, 0, 0, s)),
                   pl.BlockSpec((1, B_HEADS, ts, LANES), lambda b, s: (b, 0, s, 0)),
                   pl.BlockSpec((1, B_HEADS, ts // tk, MLA_VT_ROWS, tk),
                                lambda b, s: (b, 0, s, 0, 0))],
        out_shape=[jax.ShapeDtypeStruct((B, B_HEADS, LANES, S), BF16),
                   jax.ShapeDtypeStruct((B, B_HEADS, S, LANES), BF16),
                   jax.ShapeDtypeStruct((B, B_HEADS, S // tk, MLA_VT_ROWS, tk), BF16)],
        compiler_params=_compiler_params(("parallel", "parallel")),
        name="proj_b",
    )(x, *consts, *row_tabs, *lane_tabs)


def _attn_b_kernel(qt_ref, k_ref, vt_ref, o_ref, s0_ref, s1_ref, m_ref, acc_ref, *,
                   window_chunks):
    n_chunks, tk = vt_ref.shape[2], vt_ref.shape[4]
    tq = acc_ref.shape[2]
    for tile in range(qt_ref.shape[3] // tq):
        _attn_b_tile(qt_ref, k_ref, vt_ref, o_ref, s0_ref, s1_ref, m_ref, acc_ref,
                     slice(tile * tq, (tile + 1) * tq), n_chunks, tk, window_chunks)


def _attn_b_tile(qt_ref, k_ref, vt_ref, o_ref, s0_ref, s1_ref, m_ref, acc_ref, cols, n_chunks,
                 tk, window_chunks):
    m_ref[...] = jnp.full(m_ref.shape, MASKED, F32)
    acc_ref[...] = jnp.zeros(acc_ref.shape, F32)

    def scores(c, s_ref):
        start = pl.multiple_of(c * tk, tk)
        col_max = []
        for hh in range(m_ref.shape[0]):
            s = jnp.dot(k_ref[0, hh, pl.ds(start, tk), :], qt_ref[0, hh, :, cols],
                        preferred_element_type=F32)
            s_ref[hh] = s
            col_max.append(jnp.max(s, axis=0, keepdims=True))
        return tuple(col_max)

    def accumulate(c, s_ref, col_max):
        for hh in range(m_ref.shape[0]):
            m_old = m_ref[hh]
            m_new = jnp.maximum(m_old, col_max[hh])
            p = jnp.exp2(s_ref[hh] - m_new).astype(BF16)
            acc_ref[hh] = (acc_ref[hh] * jnp.exp2(m_old - m_new)
                           + jnp.dot(vt_ref[0, hh, c], p, preferred_element_type=F32))
            m_ref[hh] = m_new

    bufs = (s0_ref, s1_ref)

    def window(c0, col_max, last):
        for w in range(window_chunks):
            is_final = last and w == window_chunks - 1
            nxt = None if is_final else scores(c0 + w + 1, bufs[(w + 1) % 2])
            accumulate(c0 + w, bufs[w % 2], col_max)
            col_max = nxt
        return col_max

    n_windows = n_chunks // window_chunks
    col_max = lax.fori_loop(0, n_windows - 1,
                            lambda i, mx: window(i * window_chunks, mx, False),
                            scores(0, s0_ref))
    window((n_windows - 1) * window_chunks, col_max, True)
    heads = [acc_ref[hh][:B_V_DIM] / acc_ref[hh][B_V_DIM:B_V_DIM + 1]
             for hh in range(m_ref.shape[0])]
    for pi in range(len(heads) // 2):
        pair = jnp.concatenate(heads[2 * pi:2 * pi + 2], axis=0)
        o_ref[0, cols, pi * LANES:(pi + 1) * LANES] = pair.T.astype(BF16)


def _attn_b(qt, k, vt):
    B, H, _, S = qt.shape
    tq = min(MLA_Q_TILE, S)
    tstep = min(MLA_TILES_PER_STEP * tq, S)
    n_chunks, tk = vt.shape[2], vt.shape[4]
    window = min(MLA_WINDOW_CHUNKS, n_chunks)
    assert n_chunks % window == 0 and window % 2 == 0
    hs = MLA_HEADS_PER_STEP
    score_buf = pltpu.VMEM((hs, tk, tq), F32)
    return pl.pallas_call(
        functools.partial(_attn_b_kernel, window_chunks=window),
        grid=(B, H // hs, S // tstep),
        in_specs=[pl.BlockSpec((1, hs, LANES, tstep), lambda b, j, i: (b, j, 0, i)),
                  pl.BlockSpec((1, hs, S, LANES), lambda b, j, i: (b, j, 0, 0)),
                  pl.BlockSpec((1, hs, n_chunks, MLA_VT_ROWS, tk),
                               lambda b, j, i: (b, j, 0, 0, 0))],
        out_specs=pl.BlockSpec((1, tstep, hs * B_V_DIM), lambda b, j, i: (b, i, j)),
        out_shape=jax.ShapeDtypeStruct((B, S, H * B_V_DIM), BF16),
        scratch_shapes=[score_buf, score_buf, pltpu.VMEM((hs, 1, tq), F32),
                        pltpu.VMEM((hs, MLA_VT_ROWS, tq), F32)],
        compiler_params=_compiler_params(("parallel", "parallel", "arbitrary")),
        name="attn_b",
    )(qt, k, vt)


def _post_kernel(x_ref, o_ref, p_ref, wo_ref, fg_ref, wgu_ref, wd_ref, pg_ref, wpg_ref,
                 wpp_ref, out_ref):
    x = x_ref[...] + jnp.dot(o_ref[...], wo_ref[...], preferred_element_type=F32)
    h = _rms(x, fg_ref[...]).astype(BF16)
    gu = jnp.dot(h, wgu_ref[...], preferred_element_type=F32)
    gate, up = gu[:, :FFN_HIDDEN], gu[:, FFN_HIDDEN:]
    act = (gate * _sigmoid(gate) * up).astype(BF16)
    x = x + jnp.dot(act, wd_ref[...], preferred_element_type=F32)
    h = _rms(x, pg_ref[...]).astype(BF16)
    ple_gate = _sigmoid(jnp.dot(h, wpg_ref[...], preferred_element_type=F32))
    emb = jnp.dot(p_ref[0].astype(BF16), wpp_ref[...], preferred_element_type=F32)
    out_ref[...] = x + ple_gate * emb


def _post(x, o, p, layer, wo, fg, wgu, wd, pg, wpg, wpp):
    T = x.shape[0]
    tm = min(POST_TILE, T)
    consts = [wo, fg, wgu, wd, pg, wpg, wpp]
    row = lambda width: pl.BlockSpec((tm, width), lambda t: (t, 0))
    return pl.pallas_call(
        _post_kernel,
        grid=(T // tm,),
        in_specs=[row(D_MODEL), row(o.shape[1]),
                  pl.BlockSpec((1, tm, PLE_DIM), lambda t: (layer, t, 0))]
                 + [_resident(a.shape) for a in consts],
        out_specs=row(D_MODEL),
        out_shape=jax.ShapeDtypeStruct((T, D_MODEL), F32),
        compiler_params=_compiler_params(("parallel",)),
        name="post",
    )(x, o, p, *consts)


def _head_slots(w, heads, width):
    kdim = w.shape[0]
    w = w.reshape(kdim, heads, width)
    return jnp.pad(w, ((0, 0), (0, 0), (0, LANES - width))).reshape(kdim, heads * LANES)


def _lane_replicated(v):
    return jnp.broadcast_to(v[:, None], (v.shape[0], LANES))


def _prep_a(w_qkv, q_gain, k_gain, sink):
    scale = (A_HEAD_DIM ** -0.5) * LOG2E
    n_q = A_HEADS * A_HEAD_DIM
    n_qk = n_q + A_KV_HEADS * A_HEAD_DIM
    w_qt = w_qkv[:, :n_q].T.astype(BF16)
    w_k = w_qkv[:, n_q:n_qk].astype(BF16)
    w_vt = _head_slots(w_qkv[:, n_qk:], A_KV_HEADS, A_HEAD_DIM).T.astype(BF16)
    gq = _lane_replicated(q_gain * scale)
    gk = jnp.tile(k_gain, 2)[None, :]
    group = A_HEADS // A_KV_HEADS
    sink_rows = jnp.repeat(sink.reshape(A_KV_HEADS, group), BLOCK, axis=1)[:, None, :]
    return (w_qt, w_k, w_vt, gq, gk), sink_rows


def _prep_b(w_in, w_uq, w_ukv, q_gain, k_gain, q_lat_gain, kv_lat_gain, g):
    scale = (B_QK_DIM ** -0.5) * LOG2E
    half = B_ROPE_DIM // 2
    lat_cols = B_Q_RANK + B_KV_RANK
    rope_cols = w_in[:, lat_cols:]
    rope_sw_cols = jnp.concatenate([rope_cols[:, half:], rope_cols[:, :half]], axis=1)
    slot = lambda cols: jnp.pad(cols, ((0, 0), (B_NOPE_DIM, LANES - B_QK_DIM)))
    w_in_p = jnp.concatenate([w_in[:, :lat_cols], slot(rope_cols), slot(rope_sw_cols)],
                             axis=1).astype(BF16)
    w_uqt = _head_slots(w_uq, B_HEADS, B_QK_DIM).T.astype(BF16)
    ukv = w_ukv.reshape(B_KV_RANK, B_HEADS, B_NOPE_DIM + B_V_DIM)
    w_uk = _head_slots(ukv[:, :, :B_NOPE_DIM].reshape(B_KV_RANK, -1), B_HEADS, B_NOPE_DIM).astype(BF16)
    w_uv = jnp.pad(ukv[:, :, B_NOPE_DIM:], ((0, 0), (0, 0), (0, MLA_VT_ROWS - B_V_DIM)))
    w_uvt = w_uv.reshape(B_KV_RANK, -1).T.astype(BF16)
    pad = (0, LANES - B_QK_DIM)
    gq = _lane_replicated(jnp.pad(q_gain * scale, pad))
    gk = jnp.pad(k_gain, pad)[None, :]
    k_gain_sw = jnp.concatenate([k_gain[:B_NOPE_DIM], k_gain[B_NOPE_DIM + half:],
                                 k_gain[B_NOPE_DIM:B_NOPE_DIM + half]])
    gk_sw = jnp.pad(k_gain_sw, pad)[None, :]
    return [g, w_in_p, q_lat_gain[None, :], kv_lat_gain[None, :], w_uqt, w_uk, w_uvt,
            gq, gk, gk_sw]


def _trunk(x, p, mix_norm, layers_a, layers_b, a_w_o, b_w_o, ffn_norm, w_gu, w_d,
           ple_norm, w_pg, w_pp):
    B, S, _ = x.shape
    depth = p.shape[0]
    T = B * S
    p = p.reshape(depth, T, PLE_DIM)
    rows_a = _row_tables(S, A_ROT_DIM)
    lanes_a = _lane_tables(S, A_ROT_DIM, (0, HALF))
    rows_b = _row_tables(S, B_ROPE_DIM)
    c_b, lo_b, hi_b = _lane_tables(S, B_ROPE_DIM, (B_NOPE_DIM,))
    lanes_b = (c_b, lo_b + hi_b)
    for i in range(depth):
        j = i // 2
        g = mix_norm[i][None, :]
        if i % 2 == 0:
            consts, sink_rows = layers_a[j]
            qt, k, vt = _proj_a(x, g, *consts, rows_a, lanes_a)
            o = _attn_a(sink_rows, qt, k, vt)
            w_o = a_w_o[j]
        else:
            qt, k, vt = _proj_b(x, layers_b[j](g), rows_b, lanes_b)
            o = _attn_b(qt, k, vt)
            w_o = b_w_o[j]
        x = _post(x.reshape(T, D_MODEL), o.reshape(T, -1), p, i, w_o, ffn_norm[i][None, :],
                  w_gu[i], w_d[i], ple_norm[i][None, :], w_pg[i], w_pp[i]).reshape(B, S, D_MODEL)
    return x


def kernel(x_prompt, x_sample, p_prompt, p_sample, mix_norm, a_w_qkv, a_q_norm, a_k_norm, a_sink, a_w_o, b_w_in, b_q_lat_norm, b_kv_lat_norm, b_w_uq, b_w_ukv, b_q_norm, b_k_norm, b_w_o, ffn_norm, ffn_w_gate_up, ffn_w_down, ple_norm, ple_w_gate, ple_w_proj):
    layers_a = [_prep_a(a_w_qkv[j], a_q_norm[j], a_k_norm[j], a_sink[j])
                for j in range(a_w_qkv.shape[0])]
    layers_b = [functools.partial(_prep_b, b_w_in[j], b_w_uq[j], b_w_ukv[j], b_q_norm[j],
                                  b_k_norm[j], b_q_lat_norm[j], b_kv_lat_norm[j])
                for j in range(b_w_in.shape[0])]
    shared = (mix_norm, layers_a, layers_b, a_w_o.astype(BF16), b_w_o.astype(BF16),
              ffn_norm, ffn_w_gate_up.astype(BF16), ffn_w_down.astype(BF16), ple_norm,
              ple_w_gate.astype(BF16), ple_w_proj.astype(BF16))
    return (_trunk(x_prompt, p_prompt, *shared), _trunk(x_sample, p_sample, *shared))
```

```python
import functools
import math

import jax
import jax.numpy as jnp
from jax import lax
from jax.experimental import pallas as pl
from jax.experimental.pallas import tpu as pltpu

F32 = jnp.float32
BF16 = jnp.bfloat16

D_MODEL = 1024
PLE_DIM = 256
RMS_EPS = 1e-6
ROPE_THETA = 500000.0
BLOCK = 128
A_HEADS = 16
A_KV_HEADS = 4
A_HEAD_DIM = 64
A_ROT_DIM = 16
B_HEADS = 16
B_Q_RANK = 384
B_KV_RANK = 128
B_NOPE_DIM = 64
B_ROPE_DIM = 32
B_V_DIM = 64
B_QK_DIM = B_NOPE_DIM + B_ROPE_DIM
FFN_HIDDEN = 2816

LANES = 128
HALF = LANES // 2
VMEM_LIMIT_BYTES = 56 * 1024 * 1024

LOG2E = math.log2(math.e)
MASKED = -1e30
NT_DIMS = (((1,), (1,)), ((), ()))

PROJ_A_TILE = 512
PROJ_B_TILE = 1024
POST_TILE = 512
FFN_CHUNK = 256
MLA_HEADS_PER_STEP = 2
MLA_Q_TILE = 512
MLA_TILES_PER_STEP = 4
MLA_KV_TILE = 512
MLA_WINDOW_CHUNKS = 4
MLA_VT_ROWS = 80
ATTN_A_BLOCKS_PER_STEP = 64
ATTN_A_WINDOW_BLOCKS = 16


def _compiler_params(semantics):
    return pltpu.CompilerParams(dimension_semantics=semantics,
                                vmem_limit_bytes=VMEM_LIMIT_BYTES)


def _resident(shape):
    nd = len(shape)
    return pl.BlockSpec(shape, lambda *_: (0,) * nd, pipeline_mode=pl.Buffered(1))


def _rms(x, g):
    ms = jnp.mean(x * x, axis=-1, keepdims=True)
    return x * lax.rsqrt(ms + RMS_EPS) * g


def _sigmoid(x):
    return 1.0 / (1.0 + jnp.exp(-x))


def _lane_iota(shape):
    return lax.broadcasted_iota(jnp.int32, shape, len(shape) - 1)


def _across_lanes(col_block, n):
    return jnp.concatenate([col_block] * (n // LANES), axis=1)


def _rope_rows(x1, x2, cos_t, sin_t):
    return x1 * cos_t - x2 * sin_t, x1 * sin_t + x2 * cos_t


def _rope_lanes(x, c, s_from_below, s_from_above, shift):
    return (x * c + pltpu.roll(x, shift, 1) * s_from_below
            + pltpu.roll(x, LANES - shift, 1) * s_from_above)


def _angles(seq, rot_dim):
    inv = 1.0 / (ROPE_THETA ** (jnp.arange(0, rot_dim, 2, dtype=F32) / rot_dim))
    ang = jnp.arange(seq, dtype=F32)[:, None] * inv[None, :]
    return jnp.cos(ang), jnp.sin(ang)


def _lane_tables(seq, rot_dim, lane_offsets):
    half = rot_dim // 2
    cos, sin = _angles(seq, rot_dim)

    def place(block, off):
        return jnp.pad(block, ((0, 0), (off, LANES - off - half)))

    c = jnp.zeros((seq, LANES), F32)
    rotated = jnp.zeros((seq, LANES), F32)
    lo = jnp.zeros((seq, LANES), F32)
    hi = jnp.zeros((seq, LANES), F32)
    for off in lane_offsets:
        c = c + place(cos, off) + place(cos, off + half)
        rotated = rotated + place(jnp.ones_like(cos), off) + place(jnp.ones_like(cos), off + half)
        hi = hi + place(-sin, off)
        lo = lo + place(sin, off + half)
    return c + (1.0 - rotated), lo, hi


def _row_tables(seq, rot_dim):
    cos, sin = _angles(seq, rot_dim)
    return cos.T, sin.T


def _ones_row(shape):
    return (lax.broadcasted_iota(jnp.int32, shape, 0) == B_V_DIM).astype(F32)


def _proj_a_kernel(x_ref, g_ref, wqt_ref, wk_ref, wvt_ref, gq_ref, gk_ref, ct_ref, st_ref,
                   c_ref, sl_ref, sh_ref, q_ref, k_ref, vt_ref):
    x = x_ref[0]
    ts = x.shape[0]
    h = _rms(x, g_ref[...]).astype(BF16)
    q_t = lax.dot_general(wqt_ref[...], h, NT_DIMS, preferred_element_type=F32)
    kk = jnp.dot(h, wk_ref[...], preferred_element_type=F32)
    vt = lax.dot_general(wvt_ref[...], h, NT_DIMS, preferred_element_type=F32)

    gq = _across_lanes(gq_ref[...], ts)
    cos_t, sin_t = ct_ref[...], st_ref[...]
    half = A_ROT_DIM // 2
    pad = jnp.zeros((LANES - A_HEAD_DIM, BLOCK), BF16)
    for hd in range(A_HEADS):
        t = q_t[hd * A_HEAD_DIM:(hd + 1) * A_HEAD_DIM, :]
        ms = jnp.sum(t * t, axis=0, keepdims=True) / A_HEAD_DIM
        tn = t * lax.rsqrt(ms + RMS_EPS) * gq
        o1, o2 = _rope_rows(tn[:half], tn[half:A_ROT_DIM], cos_t, sin_t)
        out = jnp.concatenate([o1, o2, tn[A_ROT_DIM:]], axis=0).astype(BF16)
        for cc in range(ts // BLOCK):
            q_ref[0, hd, cc, :A_HEAD_DIM, :] = out[:, cc * BLOCK:(cc + 1) * BLOCK]
            q_ref[0, hd, cc, A_HEAD_DIM:, :] = pad

    c, sl, sh = c_ref[...], sl_ref[...], sh_ref[...]
    is_lo = _lane_iota((ts, LANES)) < HALF
    for j in range(A_KV_HEADS * A_HEAD_DIM // LANES):
        t = kk[:, j * LANES:(j + 1) * LANES]
        sq = t * t
        ss_lo = jnp.sum(jnp.where(is_lo, sq, 0.0), axis=-1, keepdims=True)
        ss_hi = jnp.sum(jnp.where(is_lo, 0.0, sq), axis=-1, keepdims=True)
        r = jnp.where(is_lo, lax.rsqrt(ss_lo / A_HEAD_DIM + RMS_EPS),
                      lax.rsqrt(ss_hi / A_HEAD_DIM + RMS_EPS))
        t = _rope_lanes(t * r * gk_ref[...], c, sl, sh, half)
        k_ref[0, 2 * j] = jnp.where(is_lo, t, 0.0).astype(BF16)
        k_ref[0, 2 * j + 1] = jnp.where(is_lo, pltpu.roll(t, HALF, 1), 0.0).astype(BF16)

    ones = _ones_row((LANES, ts))
    for g in range(A_KV_HEADS):
        blk = (vt[g * LANES:(g + 1) * LANES, :] + ones).astype(BF16)
        for cc in range(ts // BLOCK):
            vt_ref[0, g, cc] = blk[:, cc * BLOCK:(cc + 1) * BLOCK]


def _proj_a(x, g, w_qt, w_k, w_vt, gq, gk, row_tabs, lane_tabs):
    B, S, _ = x.shape
    ts = min(PROJ_A_TILE, S)
    rtab = pl.BlockSpec((A_ROT_DIM // 2, ts), lambda b, s: (0, s))
    ltab = pl.BlockSpec((ts, LANES), lambda b, s: (s, 0))
    consts = [g, w_qt, w_k, w_vt, gq, gk]
    return pl.pallas_call(
        _proj_a_kernel,
        grid=(B, S // ts),
        in_specs=[pl.BlockSpec((1, ts, D_MODEL), lambda b, s: (b, s, 0))]
                 + [_resident(a.shape) for a in consts] + [rtab, rtab, ltab, ltab, ltab],
        out_specs=[pl.BlockSpec((1, A_HEADS, ts // BLOCK, LANES, BLOCK),
                                lambda b, s: (b, 0, s, 0, 0)),
                   pl.BlockSpec((1, A_KV_HEADS, ts, LANES), lambda b, s: (b, 0, s, 0)),
                   pl.BlockSpec((1, A_KV_HEADS, ts // BLOCK, LANES, BLOCK),
                                lambda b, s: (b, 0, s, 0, 0))],
        out_shape=[jax.ShapeDtypeStruct((B, A_HEADS, S // BLOCK, LANES, BLOCK), BF16),
                   jax.ShapeDtypeStruct((B, A_KV_HEADS, S, LANES), BF16),
                   jax.ShapeDtypeStruct((B, A_KV_HEADS, S // BLOCK, LANES, BLOCK), BF16)],
        compiler_params=_compiler_params(("parallel", "parallel")),
        name="proj_a",
    )(x, *consts, *row_tabs, *lane_tabs)


def _attn_a_kernel(sink_ref, qt_ref, k_ref, vt_ref, o_ref, s0_ref, s1_ref, *, blocks_per_step):
    step = pl.program_id(2)
    nb = vt_ref.shape[2]
    group = A_HEADS // A_KV_HEADS
    width = group * BLOCK
    key = lax.broadcasted_iota(jnp.int32, (BLOCK, width), 0)
    qry = lax.broadcasted_iota(jnp.int32, (BLOCK, width), 1) & (BLOCK - 1)
    sink = sink_ref[0] * LOG2E

    def neighbours(r):
        n = step * blocks_per_step + r
        return jnp.maximum(n - 1, 0), n, jnp.minimum(n + 1, nb - 1)

    def scores(r, s_ref):
        i_prev, n, i_next = neighbours(r)
        q4t = jnp.concatenate([qt_ref[0, hd, r] for hd in range(group)], axis=1)

        def block_scores(i):
            k = k_ref[0, 0, pl.ds(pl.multiple_of(i * BLOCK, BLOCK), BLOCK), :]
            return jnp.dot(k, q4t, preferred_element_type=F32)

        s_p = jnp.where((key >= qry) & (n > 0), block_scores(i_prev), MASKED)
        s_c = block_scores(n)
        s_n = jnp.where((key <= qry) & (n < nb - 1), block_scores(i_next), MASKED)
        s_ref[0], s_ref[1], s_ref[2] = s_p, s_c, s_n
        m = jnp.max(jnp.maximum(jnp.maximum(s_p, s_c), s_n), axis=0, keepdims=True)
        return jnp.maximum(m, sink)

    def finish(r, s_ref, m):
        acc = jnp.zeros((LANES, width), F32)
        for x, i in enumerate(neighbours(r)):
            acc += jnp.dot(vt_ref[0, 0, i], jnp.exp2(s_ref[x] - m).astype(BF16),
                           preferred_element_type=F32)
        denom = acc[B_V_DIM:B_V_DIM + 1] + jnp.exp2(sink - m)
        o = acc[:B_V_DIM] / denom
        rows = pl.ds(pl.multiple_of(r * BLOCK, BLOCK), BLOCK)
        for pi in range(group // 2):
            pair = jnp.concatenate([o[:, (2 * pi) * BLOCK:(2 * pi + 1) * BLOCK],
                                    o[:, (2 * pi + 1) * BLOCK:(2 * pi + 2) * BLOCK]], axis=0)
            o_ref[0, rows, pi * LANES:(pi + 1) * LANES] = pair.T.astype(BF16)

    bufs = (s0_ref, s1_ref)
    window_blocks = min(ATTN_A_WINDOW_BLOCKS, blocks_per_step)

    def window(r0, m, last):
        for w in range(window_blocks):
            is_final = last and w == window_blocks - 1
            nxt = None if is_final else scores(r0 + w + 1, bufs[(w + 1) % 2])
            finish(r0 + w, bufs[w % 2], m)
            m = nxt
        return m

    n_windows = blocks_per_step // window_blocks
    m = lax.fori_loop(0, n_windows - 1, lambda i, m: window(i * window_blocks, m, False),
                      scores(0, s0_ref))
    window((n_windows - 1) * window_blocks, m, True)


def _attn_a(sink_rows, qt, k, vt):
    B, _, nb, _, _ = qt.shape
    S = nb * BLOCK
    r = min(ATTN_A_BLOCKS_PER_STEP, nb)
    assert r % 2 == 0 and nb % r == 0
    group = A_HEADS // A_KV_HEADS
    score_buf = pltpu.VMEM((3, BLOCK, group * BLOCK), F32)
    return pl.pallas_call(
        functools.partial(_attn_a_kernel, blocks_per_step=r),
        grid=(B, A_KV_HEADS, nb // r),
        in_specs=[pl.BlockSpec((1, 1, group * BLOCK), lambda b, g, n: (g, 0, 0)),
                  pl.BlockSpec((1, group, r, LANES, BLOCK), lambda b, g, n: (b, g, n, 0, 0)),
                  pl.BlockSpec((1, 1, S, LANES), lambda b, g, n: (b, g, 0, 0)),
                  pl.BlockSpec((1, 1, nb, LANES, BLOCK), lambda b, g, n: (b, g, 0, 0, 0))],
        out_specs=pl.BlockSpec((1, r * BLOCK, group * A_HEAD_DIM), lambda b, g, n: (b, n, g)),
        out_shape=jax.ShapeDtypeStruct((B, S, A_HEADS * A_HEAD_DIM), BF16),
        scratch_shapes=[score_buf, score_buf],
        compiler_params=_compiler_params(("parallel", "parallel", "arbitrary")),
        name="attn_a",
    )(sink_rows, qt, k, vt)


def _proj_b_kernel(x_ref, g_ref, win_ref, gql_ref, gkvl_ref, wuqt_ref, wuk_ref, wuvt_ref,
                   gq_ref, gk_ref, gksw_ref, ct_ref, st_ref, c_ref, ssg_ref,
                   q_ref, k_ref, vt_ref):
    x = x_ref[0]
    ts = x.shape[0]
    h = _rms(x, g_ref[...]).astype(BF16)
    lat = jnp.dot(h, win_ref[...], preferred_element_type=F32)
    cq = _rms(lat[:, :B_Q_RANK], gql_ref[...]).astype(BF16)
    ckv = _rms(lat[:, B_Q_RANK:B_Q_RANK + B_KV_RANK], gkvl_ref[...]).astype(BF16)
    rope_off = B_Q_RANK + B_KV_RANK
    k_rope = lat[:, rope_off:rope_off + LANES]
    k_rope_sw = lat[:, rope_off + LANES:]
    q_t = lax.dot_general(wuqt_ref[...], cq, NT_DIMS, preferred_element_type=F32)
    kn_all = jnp.dot(ckv, wuk_ref[...], preferred_element_type=F32)
    vt_all = lax.dot_general(wuvt_ref[...], ckv, NT_DIMS, preferred_element_type=F32)

    gq = _across_lanes(gq_ref[...], ts)
    cos_t, sin_t = ct_ref[...], st_ref[...]
    half = B_ROPE_DIM // 2
    gain_cos = c_ref[...] * gk_ref[...]
    rope_term = k_rope_sw * (ssg_ref[...] * gksw_ref[...])
    ones = _ones_row((MLA_VT_ROWS, ts))
    tk = vt_ref.shape[4]
    for hd in range(B_HEADS):
        sl_h = slice(hd * LANES, (hd + 1) * LANES)
        t = q_t[sl_h, :]
        ms = jnp.sum(t * t, axis=0, keepdims=True) / B_QK_DIM
        tn = t * lax.rsqrt(ms + RMS_EPS) * gq
        o1, o2 = _rope_rows(tn[B_NOPE_DIM:B_NOPE_DIM + half], tn[B_NOPE_DIM + half:B_QK_DIM],
                            cos_t, sin_t)
        q_ref[0, hd] = jnp.concatenate([tn[:B_NOPE_DIM], o1, o2, tn[B_QK_DIM:]],
                                       axis=0).astype(BF16)
        xk = kn_all[:, sl_h] + k_rope
        msk = jnp.sum(xk * xk, axis=-1, keepdims=True) / B_QK_DIM
        k_ref[0, hd] = ((xk * gain_cos + rope_term) * lax.rsqrt(msk + RMS_EPS)).astype(BF16)
        blk = (vt_all[hd * MLA_VT_ROWS:(hd + 1) * MLA_VT_ROWS, :] + ones).astype(BF16)
        for cc in range(ts // tk):
            vt_ref[0, hd, cc] = blk[:, cc * tk:(cc + 1) * tk]


def _proj_b(x, consts, row_tabs, lane_tabs):
    B, S, _ = x.shape
    ts = min(PROJ_B_TILE, S)
    tk = min(MLA_KV_TILE, max(LANES, S // 8))
    rtab = pl.BlockSpec((B_ROPE_DIM // 2, ts), lambda b, s: (0, s))
    ltab = pl.BlockSpec((ts, LANES), lambda b, s: (s, 0))
    return pl.pallas_call(
        _proj_b_kernel,
        grid=(B, S // ts),
        in_specs=[pl.BlockSpec((1, ts, D_MODEL), lambda b, s: (b, s, 0))]
                 + [_resident(a.shape) for a in consts] + [rtab, rtab, ltab, ltab],
        out_specs=[pl.BlockSpec((1, B_HEADS, LANES, ts), lambda b, s: (b, 0, 0, s)),
                   pl.BlockSpec((1, B_HEADS, ts, LANES), lambda b, s: (b, 0, s, 0)),
                   pl.BlockSpec((1, B_HEADS, ts // tk, MLA_VT_ROWS, tk),
                                lambda b, s: (b, 0, s, 0, 0))],
        out_shape=[jax.ShapeDtypeStruct((B, B_HEADS, LANES, S), BF16),
                   jax.ShapeDtypeStruct((B, B_HEADS, S, LANES), BF16),
                   jax.ShapeDtypeStruct((B, B_HEADS, S // tk, MLA_VT_ROWS, tk), BF16)],
        compiler_params=_compiler_params(("parallel", "parallel")),
        name="proj_b",
    )(x, *consts, *row_tabs, *lane_tabs)


def _attn_b_kernel(qt_ref, k_ref, vt_ref, o_ref, s0_ref, s1_ref, m_ref, acc_ref, *,
                   window_chunks):
    n_chunks, tk = vt_ref.shape[2], vt_ref.shape[4]
    tq = acc_ref.shape[2]
    for tile in range(qt_ref.shape[3] // tq):
        _attn_b_tile(qt_ref, k_ref, vt_ref, o_ref, s0_ref, s1_ref, m_ref, acc_ref,
                     slice(tile * tq, (tile + 1) * tq), n_chunks, tk, window_chunks)


def _attn_b_tile(qt_ref, k_ref, vt_ref, o_ref, s0_ref, s1_ref, m_ref, acc_ref, cols, n_chunks,
                 tk, window_chunks):
    m_ref[...] = jnp.full(m_ref.shape, MASKED, F32)
    acc_ref[...] = jnp.zeros(acc_ref.shape, F32)

    def scores(c, s_ref):
        start = pl.multiple_of(c * tk, tk)
        col_max = []
        for hh in range(m_ref.shape[0]):
            s = jnp.dot(k_ref[0, hh, pl.ds(start, tk), :], qt_ref[0, hh, :, cols],
                        preferred_element_type=F32)
            s_ref[hh] = s
            col_max.append(jnp.max(s, axis=0, keepdims=True))
        return tuple(col_max)

    def accumulate(c, s_ref, col_max):
        for hh in range(m_ref.shape[0]):
            m_old = m_ref[hh]
            m_new = jnp.maximum(m_old, col_max[hh])
            p = jnp.exp2(s_ref[hh] - m_new).astype(BF16)
            acc_ref[hh] = (acc_ref[hh] * jnp.exp2(m_old - m_new)
                           + jnp.dot(vt_ref[0, hh, c], p, preferred_element_type=F32))
            m_ref[hh] = m_new

    bufs = (s0_ref, s1_ref)

    def window(c0, col_max, last):
        for w in range(window_chunks):
            is_final = last and w == window_chunks - 1
            nxt = None if is_final else scores(c0 + w + 1, bufs[(w + 1) % 2])
            accumulate(c0 + w, bufs[w % 2], col_max)
            col_max = nxt
        return col_max

    n_windows = n_chunks // window_chunks
    col_max = lax.fori_loop(0, n_windows - 1,
                            lambda i, mx: window(i * window_chunks, mx, False),
                            scores(0, s0_ref))
    window((n_windows - 1) * window_chunks, col_max, True)
    heads = [acc_ref[hh][:B_V_DIM] / acc_ref[hh][B_V_DIM:B_V_DIM + 1]
             for hh in range(m_ref.shape[0])]
    for pi in range(len(heads) // 2):
        pair = jnp.concatenate(heads[2 * pi:2 * pi + 2], axis=0)
        o_ref[0, cols, pi * LANES:(pi + 1) * LANES] = pair.T.astype(BF16)


def _attn_b(qt, k, vt):
    B, H, _, S = qt.shape
    tq = min(MLA_Q_TILE, S)
    tstep = min(MLA_TILES_PER_STEP * tq, S)
    n_chunks, tk = vt.shape[2], vt.shape[4]
    window = min(MLA_WINDOW_CHUNKS, n_chunks)
    assert n_chunks % window == 0 and window % 2 == 0
    hs = MLA_HEADS_PER_STEP
    score_buf = pltpu.VMEM((hs, tk, tq), F32)
    return pl.pallas_call(
        functools.partial(_attn_b_kernel, window_chunks=window),
        grid=(B, H // hs, S // tstep),
        in_specs=[pl.BlockSpec((1, hs, LANES, tstep), lambda b, j, i: (b, j, 0, i)),
                  pl.BlockSpec((1, hs, S, LANES), lambda b, j, i: (b, j, 0, 0)),
                  pl.BlockSpec((1, hs, n_chunks, MLA_VT_ROWS, tk),
                               lambda b, j, i: (b, j, 0, 0, 0))],
        out_specs=pl.BlockSpec((1, tstep, hs * B_V_DIM), lambda b, j, i: (b, i, j)),
        out_shape=jax.ShapeDtypeStruct((B, S, H * B_V_DIM), BF16),
        scratch_shapes=[score_buf, score_buf, pltpu.VMEM((hs, 1, tq), F32),
                        pltpu.VMEM((hs, MLA_VT_ROWS, tq), F32)],
        compiler_params=_compiler_params(("parallel", "parallel", "arbitrary")),
        name="attn_b",
    )(qt, k, vt)


def _post_kernel(x_ref, o_ref, p_ref, wo_ref, fg_ref, wgu_ref, wd_ref, pg_ref, wpg_ref,
                 wpp_ref, out_ref):
    x = x_ref[...] + jnp.dot(o_ref[...], wo_ref[...], preferred_element_type=F32)
    h = _rms(x, fg_ref[...]).astype(BF16)
    ffn = jnp.zeros_like(x)
    for c0 in range(0, FFN_HIDDEN, FFN_CHUNK):
        gate = jnp.dot(h, wgu_ref[:, c0:c0 + FFN_CHUNK], preferred_element_type=F32)
        up = jnp.dot(h, wgu_ref[:, FFN_HIDDEN + c0:FFN_HIDDEN + c0 + FFN_CHUNK],
                     preferred_element_type=F32)
        act = (gate * _sigmoid(gate) * up).astype(BF16)
        ffn += jnp.dot(act, wd_ref[c0:c0 + FFN_CHUNK, :], preferred_element_type=F32)
    x = x + ffn
    h = _rms(x, pg_ref[...]).astype(BF16)
    ple_gate = _sigmoid(jnp.dot(h, wpg_ref[...], preferred_element_type=F32))
    emb = jnp.dot(p_ref[0].astype(BF16), wpp_ref[...], preferred_element_type=F32)
    out_ref[...] = x + ple_gate * emb


def _post(x, o, p, layer, wo, fg, wgu, wd, pg, wpg, wpp):
    T = x.shape[0]
    tm = min(POST_TILE, T)
    consts = [wo, fg, wgu, wd, pg, wpg, wpp]
    row = lambda width: pl.BlockSpec((tm, width), lambda t: (t, 0))
    return pl.pallas_call(
        _post_kernel,
        grid=(T // tm,),
        in_specs=[row(D_MODEL), row(o.shape[1]),
                  pl.BlockSpec((1, tm, PLE_DIM), lambda t: (layer, t, 0))]
                 + [_resident(a.shape) for a in consts],
        out_specs=row(D_MODEL),
        out_shape=jax.ShapeDtypeStruct((T, D_MODEL), F32),
        compiler_params=_compiler_params(("parallel",)),
        name="post",
    )(x, o, p, *consts)


def _head_slots(w, heads, width):
    kdim = w.shape[0]
    w = w.reshape(kdim, heads, width)
    return jnp.pad(w, ((0, 0), (0, 0), (0, LANES - width))).reshape(kdim, heads * LANES)


def _lane_replicated(v):
    return jnp.broadcast_to(v[:, None], (v.shape[0], LANES))


def _prep_a(w_qkv, q_gain, k_gain, sink):
    scale = (A_HEAD_DIM ** -0.5) * LOG2E
    n_q = A_HEADS * A_HEAD_DIM
    n_qk = n_q + A_KV_HEADS * A_HEAD_DIM
    w_qt = w_qkv[:, :n_q].T.astype(BF16)
    w_k = w_qkv[:, n_q:n_qk].astype(BF16)
    w_vt = _head_slots(w_qkv[:, n_qk:], A_KV_HEADS, A_HEAD_DIM).T.astype(BF16)
    gq = _lane_replicated(q_gain * scale)
    gk = jnp.tile(k_gain, 2)[None, :]
    group = A_HEADS // A_KV_HEADS
    sink_rows = jnp.repeat(sink.reshape(A_KV_HEADS, group), BLOCK, axis=1)[:, None, :]
    return (w_qt, w_k, w_vt, gq, gk), sink_rows


def _prep_b(w_in, w_uq, w_ukv, q_gain, k_gain, q_lat_gain, kv_lat_gain, g):
    scale = (B_QK_DIM ** -0.5) * LOG2E
    half = B_ROPE_DIM // 2
    lat_cols = B_Q_RANK + B_KV_RANK
    rope_cols = w_in[:, lat_cols:]
    rope_sw_cols = jnp.concatenate([rope_cols[:, half:], rope_cols[:, :half]], axis=1)
    slot = lambda cols: jnp.pad(cols, ((0, 0), (B_NOPE_DIM, LANES - B_QK_DIM)))
    w_in_p = jnp.concatenate([w_in[:, :lat_cols], slot(rope_cols), slot(rope_sw_cols)],
                             axis=1).astype(BF16)
    w_uqt = _head_slots(w_uq, B_HEADS, B_QK_DIM).T.astype(BF16)
    ukv = w_ukv.reshape(B_KV_RANK, B_HEADS, B_NOPE_DIM + B_V_DIM)
    w_uk = _head_slots(ukv[:, :, :B_NOPE_DIM].reshape(B_KV_RANK, -1), B_HEADS, B_NOPE_DIM).astype(BF16)
    w_uv = jnp.pad(ukv[:, :, B_NOPE_DIM:], ((0, 0), (0, 0), (0, MLA_VT_ROWS - B_V_DIM)))
    w_uvt = w_uv.reshape(B_KV_RANK, -1).T.astype(BF16)
    pad = (0, LANES - B_QK_DIM)
    gq = _lane_replicated(jnp.pad(q_gain * scale, pad))
    gk = jnp.pad(k_gain, pad)[None, :]
    k_gain_sw = jnp.concatenate([k_gain[:B_NOPE_DIM], k_gain[B_NOPE_DIM + half:],
                                 k_gain[B_NOPE_DIM:B_NOPE_DIM + half]])
    gk_sw = jnp.pad(k_gain_sw, pad)[None, :]
    return [g, w_in_p, q_lat_gain[None, :], kv_lat_gain[None, :], w_uqt, w_uk, w_uvt,
            gq, gk, gk_sw]


def _trunk(x, p, mix_norm, layers_a, layers_b, a_w_o, b_w_o, ffn_norm, w_gu, w_d,
           ple_norm, w_pg, w_pp):
    B, S, _ = x.shape
    depth = p.shape[0]
    T = B * S
    p = p.reshape(depth, T, PLE_DIM)
    rows_a = _row_tables(S, A_ROT_DIM)
    lanes_a = _lane_tables(S, A_ROT_DIM, (0, HALF))
    rows_b = _row_tables(S, B_ROPE_DIM)
    c_b, lo_b, hi_b = _lane_tables(S, B_ROPE_DIM, (B_NOPE_DIM,))
    lanes_b = (c_b, lo_b + hi_b)
    for i in range(depth):
        j = i // 2
        g = mix_norm[i][None, :]
        if i % 2 == 0:
            consts, sink_rows = layers_a[j]
            qt, k, vt = _proj_a(x, g, *consts, rows_a, lanes_a)
            o = _attn_a(sink_rows, qt, k, vt)
            w_o = a_w_o[j]
        else:
            qt, k, vt = _proj_b(x, layers_b[j](g), rows_b, lanes_b)
            o = _attn_b(qt, k, vt)
            w_o = b_w_o[j]
        x = _post(x.reshape(T, D_MODEL), o.reshape(T, -1), p, i, w_o, ffn_norm[i][None, :],
                  w_gu[i], w_d[i], ple_norm[i][None, :], w_pg[i], w_pp[i]).reshape(B, S, D_MODEL)
    return x


def kernel(x_prompt, x_sample, p_prompt, p_sample, mix_norm, a_w_qkv, a_q_norm, a_k_norm, a_sink, a_w_o, b_w_in, b_q_lat_norm, b_kv_lat_norm, b_w_uq, b_w_ukv, b_q_norm, b_k_norm, b_w_o, ffn_norm, ffn_w_gate_up, ffn_w_down, ple_norm, ple_w_gate, ple_w_proj):
    layers_a = [_prep_a(a_w_qkv[j], a_q_norm[j], a_k_norm[j], a_sink[j])
                for j in range(a_w_qkv.shape[0])]
    layers_b = [functools.partial(_prep_b, b_w_in[j], b_w_uq[j], b_w_ukv[j], b_q_norm[j],
                                  b_k_norm[j], b_q_lat_norm[j], b_kv_lat_norm[j])
                for j in range(b_w_in.shape[0])]
    shared = (mix_norm, layers_a, layers_b, a_w_o.astype(BF16), b_w_o.astype(BF16),
              ffn_norm, ffn_w_gate_up.astype(BF16), ffn_w_down.astype(BF16), ple_norm,
              ple_w_gate.astype(BF16), ple_w_proj.astype(BF16))
    return (_trunk(x_prompt, p_prompt, *shared), _trunk(x_sample, p_sample, *shared))
```
